```python
import math
import jax, jax.numpy as jnp
from jax import lax
import numpy as np

D_MODEL = 1024
BATCH = 2
SEQ = 16384
DEPTH = 4
DEC_BATCH = 2
DEC_SEQ = 8192
PAST_LEN = 128

WINDOW_DILATIONS = ((128, 1), (512, 4), (2048, 16))
N_ATTN_GROUPS = len(WINDOW_DILATIONS)
HEADS_PER_GROUP = 16
HEAD_DIM = 64
ATTN_WIDTH = HEADS_PER_GROUP * HEAD_DIM
QKV_COLS = N_ATTN_GROUPS * 3 * ATTN_WIDTH
ROPE_THETA = 10000.0
Q_BLOCK = 128
NEG_INF = -1e30
POOL_WINDOWS = (2, 4, 8, 16)
N_POOL_GROUPS = len(POOL_WINDOWS)
POOL_CH = D_MODEL // N_POOL_GROUPS
N_EXPERT_GROUPS = 4
EXPERTS_PER_GROUP = 8
N_EXPERTS = N_EXPERT_GROUPS * EXPERTS_PER_GROUP
TOP_K = 2
D_EXPERT = 512
MOE_BLOCK = 128
N_ATTN_LAYERS = (DEPTH + 1) // 2
N_POOL_LAYERS = DEPTH // 2
RMS_EPS = 1e-6

kernel_name = "hybrid_dilated_attn_pool_hmoe_encoder"


def rmsnorm(x, g):
    xf = x.astype(jnp.float32)
    y = xf * lax.rsqrt(jnp.mean(xf * xf, axis=-1, keepdims=True) + RMS_EPS) * g.astype(jnp.float32)
    return y.astype(x.dtype)


def rope_tables(seq_len):
    inv_freq = 1.0 / (ROPE_THETA ** (jnp.arange(0, HEAD_DIM, 2, dtype=jnp.float32) / HEAD_DIM))
    ang = jnp.arange(seq_len, dtype=jnp.float32)[:, None] * inv_freq[None, :]
    return jnp.cos(ang), jnp.sin(ang)


def apply_rope(x, cos, sin):
    xf = x.astype(jnp.float32)
    x1, x2 = jnp.split(xf, 2, axis=-1)
    c = cos[None, :, None, :]
    s = sin[None, :, None, :]
    return jnp.concatenate([x1 * c - x2 * s, x1 * s + x2 * c], axis=-1).astype(x.dtype)


def local_window_attn(q, k, v, half):
    n, L, hd = q.shape
    nb = -(-L // Q_BLOCK)
    Lp = nb * Q_BLOCK
    W = Q_BLOCK + 2 * half
    qp = jnp.pad(q, ((0, 0), (0, Lp - L), (0, 0)))
    kp = jnp.pad(k, ((0, 0), (half, Lp - L + half), (0, 0)))
    vp = jnp.pad(v, ((0, 0), (half, Lp - L + half), (0, 0)))
    idx = jnp.arange(nb)[:, None] * Q_BLOCK + jnp.arange(W)[None, :]
    kw = kp[:, idx]
    vw = vp[:, idx]
    qb = qp.reshape(n, nb, Q_BLOCK, hd)
    s = jnp.einsum('nbqd,nbkd->nbqk', qb, kw, preferred_element_type=jnp.float32) * (hd ** -0.5)
    rel = jnp.arange(W)[None, :] - jnp.arange(Q_BLOCK)[:, None]
    kpos = idx - half
    mask = ((rel >= 0) & (rel <= 2 * half))[None] & ((kpos >= 0) & (kpos < L))[:, None, :]
    s = jnp.where(mask[None], s, NEG_INF)
    m = jnp.max(s, axis=-1, keepdims=True)
    p = jnp.exp(s - m)
    l = jnp.sum(p, axis=-1)
    o = jnp.einsum('nbqk,nbkd->nbqd', p.astype(v.dtype), vw, preferred_element_type=jnp.float32) / l[..., None]
    lse = m[..., 0] + jnp.log(l)
    return o.reshape(n, Lp, hd)[:, :L].astype(q.dtype), lse.reshape(n, Lp)[:, :L]


def dilated_window_attn(q, k, v, dil, half):
    b, S, h, hd = q.shape
    L = S // dil

    def split(t):
        return t.reshape(b, L, dil, h, hd).transpose(0, 2, 3, 1, 4).reshape(b * dil * h, L, hd)

    o, lse = local_window_attn(split(q), split(k), split(v), half)
    o = o.reshape(b, dil, h, L, hd).transpose(0, 3, 1, 2, 4).reshape(b, S, h, hd)
    lse = lse.reshape(b, dil, h, L).transpose(0, 3, 1, 2).reshape(b, S, h)
    return o, lse


def attn_mixer(h, w_qkv, w_o):
    b, S, _ = h.shape
    qkv = (h @ w_qkv).reshape(b, S, N_ATTN_GROUPS, 3, HEADS_PER_GROUP, HEAD_DIM)
    cos, sin = rope_tables(S)
    outs, lses = [], []
    for g, (window, dil) in enumerate(WINDOW_DILATIONS):
        q = apply_rope(qkv[:, :, g, 0], cos, sin)
        k = apply_rope(qkv[:, :, g, 1], cos, sin)
        v = qkv[:, :, g, 2]
        o, lse = dilated_window_attn(q, k, v, dil, window // (2 * dil))
        outs.append(o)
        lses.append(lse)
    o = jnp.stack(outs, axis=3)
    alpha = jax.nn.softmax(jnp.stack(lses, axis=-1), axis=-1)
    merged = jnp.einsum('bshgd,bshg->bshd', o, alpha.astype(o.dtype))
    return merged.reshape(b, S, ATTN_WIDTH) @ w_o


def pool_mixer(h, w_pool, pool_scale):
    b, S, D = h.shape
    csum = jnp.concatenate([jnp.zeros((b, 1, D), jnp.float32),
                            jnp.cumsum(h.astype(jnp.float32), axis=1)], axis=1)
    t = jnp.arange(S)
    outs = []
    for g, w in enumerate(POOL_WINDOWS):
        sl = slice(g * POOL_CH, (g + 1) * POOL_CH)
        lo = jnp.clip(t - w // 2, 0, S)
        hi = jnp.clip(t - w // 2 + w, 0, S)
        c = csum[:, :, sl]
        mean = (c[:, hi] - c[:, lo]) / (hi - lo).astype(jnp.float32)[None, :, None]
        diff = (mean - h[:, :, sl].astype(jnp.float32)).astype(h.dtype)
        outs.append(diff @ w_pool[g])
    return jnp.concatenate(outs, axis=-1) * pool_scale


def hier_moe(h, rgw, rgb, rew, reb, wg, wu, wd):
    T = h.shape[0]
    gprob = jax.nn.softmax((h @ rgw).astype(jnp.float32) + rgb.astype(jnp.float32), axis=-1)
    gp, gi = lax.top_k(gprob, 1)
    gp, gi = gp[:, 0], gi[:, 0]
    elog = ((h @ rew).astype(jnp.float32) + reb.astype(jnp.float32)).reshape(T, N_EXPERT_GROUPS, EXPERTS_PER_GROUP)
    gidx = jnp.broadcast_to(gi[:, None, None], (T, 1, EXPERTS_PER_GROUP))
    eprob = jax.nn.softmax(jnp.take_along_axis(elog, gidx, axis=1)[:, 0], axis=-1)
    ep, ei = lax.top_k(eprob, TOP_K)
    gates = gp[:, None] * ep / jnp.sum(ep, axis=-1, keepdims=True)
    expert_id = gi[:, None] * EXPERTS_PER_GROUP + ei

    A = T * TOP_K
    e_flat = expert_id.reshape(A)
    tok = jnp.repeat(jnp.arange(T), TOP_K)
    g_flat = gates.reshape(A)
    order = jnp.argsort(e_flat, stable=True)
    e_s, tok_s, g_s = e_flat[order], tok[order], g_flat[order]
    counts = jax.ops.segment_sum(jnp.ones((A,), jnp.int32), e_flat, num_segments=N_EXPERTS)
    starts = jnp.cumsum(counts) - counts
    pcounts = (counts + MOE_BLOCK - 1) // MOE_BLOCK * MOE_BLOCK
    pends = jnp.cumsum(pcounts)
    pstarts = pends - pcounts
    dest = pstarts[e_s] + (jnp.arange(A) - starts[e_s])
    nb = -(-A // MOE_BLOCK) + N_EXPERTS
    xbuf = jnp.zeros((nb * MOE_BLOCK, h.shape[1]), h.dtype).at[dest].set(h[tok_s])
    block_e = jnp.clip(jnp.searchsorted(pends, jnp.arange(nb) * MOE_BLOCK, side='right'), 0, N_EXPERTS - 1)

    def expert_block(args):
        xb, e = args
        return (jax.nn.silu(xb @ wg[e]) * (xb @ wu[e])) @ wd[e]

    ybuf = lax.map(expert_block, (xbuf.reshape(nb, MOE_BLOCK, -1), block_e)).reshape(nb * MOE_BLOCK, -1)
    y_s = (ybuf[dest].astype(jnp.float32) * g_s[:, None]).astype(h.dtype)
    return jax.ops.segment_sum(y_s, tok_s, num_segments=T)


def trunk(x, mixer_norm, w_qkv, w_o, w_pool, pool_scale, ffn_norm,
          router_group_w, router_group_b, router_expert_w, router_expert_b,
          w_gate, w_up, w_down, final_norm):
    b, S, D = x.shape
    for i in range(DEPTH):
        h = rmsnorm(x, mixer_norm[i])
        if i % 2 == 0:
            x = x + attn_mixer(h, w_qkv[i // 2], w_o[i // 2])
        else:
            x = x + pool_mixer(h, w_pool[i // 2], pool_scale[i // 2])
        h = rmsnorm(x, ffn_norm[i]).reshape(b * S, D)
        x = x + hier_moe(h, router_group_w[i], router_group_b[i], router_expert_w[i],
                         router_expert_b[i], w_gate[i], w_up[i], w_down[i]).reshape(b, S, D)
    return rmsnorm(x, final_norm)


def setup_inputs(seed: int = 0) -> dict:
    key = jax.random.key(seed)
    ks = jax.random.split(key, 16)
    nrm = lambda k, shape, scale: jax.random.normal(k, shape, jnp.float32) * scale
    D = D_MODEL
    return {
        "x_prompt": nrm(ks[0], (BATCH, SEQ, D), 1.0),
        "x_sample": nrm(ks[1], (DEC_BATCH, DEC_SEQ, D), 1.0),
        "mixer_norm": 1.0 + nrm(ks[2], (DEPTH, D), 0.1),
        "w_qkv": nrm(ks[3], (N_ATTN_LAYERS, D, QKV_COLS), D ** -0.5),
        "w_o": nrm(ks[4], (N_ATTN_LAYERS, ATTN_WIDTH, D), ATTN_WIDTH ** -0.5),
        "w_pool": nrm(ks[5], (N_POOL_LAYERS, N_POOL_GROUPS, POOL_CH, POOL_CH), POOL_CH ** -0.5),
        "pool_scale": 1.0 + nrm(ks[6], (N_POOL_LAYERS, D), 0.1),
        "ffn_norm": 1.0 + nrm(ks[7], (DEPTH, D), 0.1),
        "router_group_w": nrm(ks[8], (DEPTH, D, N_EXPERT_GROUPS), D ** -0.5),
        "router_group_b": nrm(ks[9], (DEPTH, N_EXPERT_GROUPS), 0.01),
        "router_expert_w": nrm(ks[10], (DEPTH, D, N_EXPERTS), D ** -0.5),
        "router_expert_b": nrm(ks[11], (DEPTH, N_EXPERTS), 0.01),
        "w_gate": nrm(ks[12], (DEPTH, N_EXPERTS, D, D_EXPERT), D ** -0.5),
        "w_up": nrm(ks[13], (DEPTH, N_EXPERTS, D, D_EXPERT), D ** -0.5),
        "w_down": nrm(ks[14], (DEPTH, N_EXPERTS, D_EXPERT, D), D_EXPERT ** -0.5),
        "final_norm": 1.0 + nrm(ks[15], (D,), 0.1),
    }


def reference(x_prompt, x_sample, mixer_norm, w_qkv, w_o, w_pool, pool_scale, ffn_norm,
              router_group_w, router_group_b, router_expert_w, router_expert_b,
              w_gate, w_up, w_down, final_norm):
    y_prompt = trunk(x_prompt, mixer_norm, w_qkv, w_o, w_pool, pool_scale, ffn_norm,
                     router_group_w, router_group_b, router_expert_w, router_expert_b,
                     w_gate, w_up, w_down, final_norm)
    y_sample = trunk(x_sample, mixer_norm, w_qkv, w_o, w_pool, pool_scale, ffn_norm,
                     router_group_w, router_group_b, router_expert_w, router_expert_b,
                     w_gate, w_up, w_down, final_norm)
    return (y_prompt, y_sample)
```

```python
import functools

import jax
import jax.numpy as jnp
from jax import lax
from jax.experimental import pallas as pl
from jax.experimental.pallas import tpu as pltpu

F32 = jnp.float32
BF16 = jnp.bfloat16

D_MODEL = 1024
DEPTH = 4
DILATIONS = (1, 4, 16)
HALF = 64
N_HEADS = 16
HEAD_DIM = 64
Q_BLOCK = 128
ROPE_THETA = 10000.0
NEG_INF = -1e30
POOL_WINDOWS = (2, 4, 8, 16)
POOL_CH = 256
N_GROUPS = 4
EPG = 8
N_EXPERTS = 32
TOP_K = 2
D_EXPERT = 512
MOE_BLOCK = 128
RMS_EPS = 1e-6
ROUTER_LANES = 128
HALO = 8
VMEM_LIMIT = 48 * 1024 * 1024


def _rms(x, g):
    return x * lax.rsqrt(jnp.mean(x * x, axis=-1, keepdims=True) + RMS_EPS) * g


def _cparams(sem):
    return pltpu.CompilerParams(dimension_semantics=sem, vmem_limit_bytes=VMEM_LIMIT)


def _qkv_kernel(*refs, n_add, d, tm, emit_x):
    x_ref = refs[0]
    add_refs = refs[1:1 + n_add]
    g_ref, w_ref, cos_ref, sin_ref = refs[1 + n_add:5 + n_add]
    outs = refs[5 + n_add:]
    if emit_x:
        xs_ref, out_ref, hperm, hf = outs
    else:
        out_ref, hperm, hf = outs
    c = pl.program_id(1)
    n = tm // d

    @pl.when(c == 0)
    def _():
        xs = x_ref[...]
        for a in add_refs:
            xs = xs + a[...]
        if emit_x:
            xs_ref[...] = xs
        h = _rms(xs, g_ref[...])
        if d == 1:
            hperm[...] = h.astype(BF16)
        else:
            for j in range(D_MODEL // 128):
                hf[j] = h[:, 128 * j:128 * (j + 1)]
            for r in range(d):
                for j in range(D_MODEL // 128):
                    hperm[r * n:(r + 1) * n, 128 * j:128 * (j + 1)] = hf[j, pl.ds(r, n, stride=d), :].astype(BF16)

    acc = jnp.dot(hperm[...], w_ref[...], preferred_element_type=F32)

    def store(res):
        for r in range(d):
            out_ref[r, :, :] = res[r * n:(r + 1) * n, :]

    def rope(scale):
        lane = lax.broadcasted_iota(jnp.int32, (tm, 128), 1)
        low = (lane % HEAD_DIM) < (HEAD_DIM // 2)
        cs = cos_ref[...]
        sn = sin_ref[...]
        parts = []
        for j in range(D_MODEL // 128):
            xj = acc[:, 128 * j:128 * (j + 1)]
            partner = jnp.where(low, pltpu.roll(xj, 128 - HEAD_DIM // 2, 1), pltpu.roll(xj, HEAD_DIM // 2, 1))
            o = xj * cs + partner * sn
            if scale != 1.0:
                o = o * scale
            parts.append(o.astype(BF16))
        store(jnp.concatenate(parts, axis=1))

    @pl.when(c == 0)
    def _():
        rope(HEAD_DIM ** -0.5)

    @pl.when(c == 1)
    def _():
        rope(1.0)

    @pl.when(c == 2)
    def _():
        store(acc.astype(BF16))


def _qkv_call(x, adds, g, w, cos, sin, *, B, S, d, emit_x, tm=512):
    T = B * S
    nI = S // tm
    L = S // d
    n = tm // d
    n_add = len(adds)
    row = pl.BlockSpec((tm, D_MODEL), lambda i, c: (i, 0))
    in_specs = [row]
    T_rows = T // tm
    for k in range(n_add):
        in_specs.append(pl.BlockSpec((tm, D_MODEL), lambda i, c, k=k: (k * T_rows + i, 0)))
    in_specs += [
        pl.BlockSpec((1, D_MODEL), lambda i, c: (0, 0)),
        pl.BlockSpec((D_MODEL, D_MODEL), lambda i, c: (0, c)),
        pl.BlockSpec((tm, 128), lambda i, c: (i % nI, 0)),
        pl.BlockSpec((tm, 128), lambda i, c: (i % nI, 0)),
    ]
    out_spec = pl.BlockSpec((None, d, n, D_MODEL), lambda i, c: (i // nI, 0, i % nI, c))
    out_shape = jax.ShapeDtypeStruct((B, d, L, 3 * D_MODEL), BF16)
    if emit_x:
        out_specs = [row, out_spec]
        out_shapes = [jax.ShapeDtypeStruct((T, D_MODEL), F32), out_shape]
    else:
        out_specs = [out_spec]
        out_shapes = [out_shape]
    res = pl.pallas_call(
        functools.partial(_qkv_kernel, n_add=n_add, d=d, tm=tm, emit_x=emit_x),
        grid=(T // tm, 3),
        in_specs=in_specs,
        out_specs=out_specs,
        out_shape=out_shapes,
        scratch_shapes=[pltpu.VMEM((tm, D_MODEL), BF16), pltpu.VMEM((D_MODEL // 128, tm, 128), F32)],
        compiler_params=_cparams(("parallel", "arbitrary")),
    )(x, *adds, g, w, cos, sin)
    return res


def _attn_kernel(q_ref, kp_ref, kc_ref, kn_ref, vp_ref, vc_ref, vn_ref, o_ref, lse_ref, kbuf, vbuf, *, L):
    i = pl.program_id(1)
    W = Q_BLOCK + 2 * HALF
    kbuf[0:HALF, :] = kp_ref[...]
    kbuf[HALF:HALF + Q_BLOCK, :] = kc_ref[...]
    kbuf[HALF + Q_BLOCK:W, :] = kn_ref[...]
    vbuf[0:HALF, :] = vp_ref[...]
    vbuf[HALF:HALF + Q_BLOCK, :] = vc_ref[...]
    vbuf[HALF + Q_BLOCK:W, :] = vn_ref[...]
    row = lax.broadcasted_iota(jnp.int32, (Q_BLOCK, W), 0)
    col = lax.broadcasted_iota(jnp.int32, (Q_BLOCK, W), 1)
    rel = col - row
    kpos = i * Q_BLOCK - HALF + col
    mask = (rel >= 0) & (rel <= 2 * HALF) & (kpos >= 0) & (kpos < L)
    lane = lax.broadcasted_iota(jnp.int32, (Q_BLOCK, 128), 1)
    lse_tile = jnp.zeros((Q_BLOCK, 128), F32)
    for h in range(N_HEADS):
        sl = slice(h * HEAD_DIM, (h + 1) * HEAD_DIM)
        s = lax.dot_general(q_ref[:, sl], kbuf[:, sl], (((1,), (1,)), ((), ())), preferred_element_type=F32)
        s = jnp.where(mask, s, NEG_INF)
        m = jnp.max(s, axis=-1, keepdims=True)
        p = jnp.exp(s - m)
        l = jnp.sum(p, axis=-1, keepdims=True)
        o = jnp.dot(p.astype(BF16), vbuf[:, sl], preferred_element_type=F32) / l
        o_ref[:, sl] = o.astype(BF16)
        lse_tile = jnp.where(lane == h, m + jnp.log(l), lse_tile)
    lse_ref[...] = lse_tile


def _attn_call(qkv, *, L):
    NS = qkv.shape[0]
    nq = L // Q_BLOCK
    nh = L // HALF
    q_spec = pl.BlockSpec((None, Q_BLOCK, D_MODEL), lambda s, i: (s, i, 0))

    def cur(cb):
        return pl.BlockSpec((None, Q_BLOCK, D_MODEL), lambda s, i: (s, i, cb))

    def prev(cb):
        return pl.BlockSpec((None, HALF, D_MODEL), lambda s, i: (s, jnp.maximum(2 * i - 1, 0), cb))

    def nxt(cb):
        return pl.BlockSpec((None, HALF, D_MODEL), lambda s, i: (s, jnp.minimum(2 * i + 2, nh - 1), cb))

    return pl.pallas_call(
        functools.partial(_attn_kernel, L=L),
        grid=(NS, nq),
        in_specs=[q_spec, prev(1), cur(1), nxt(1), prev(2), cur(2), nxt(2)],
        out_specs=[pl.BlockSpec((None, Q_BLOCK, D_MODEL), lambda s, i: (s, i, 0)),
                   pl.BlockSpec((None, Q_BLOCK, 128), lambda s, i: (s, i, 0))],
        out_shape=[jax.ShapeDtypeStruct((NS, L, D_MODEL), BF16),
                   jax.ShapeDtypeStruct((NS, L, 128), F32)],
        scratch_shapes=[pltpu.VMEM((Q_BLOCK + 2 * HALF, D_MODEL), BF16),
                        pltpu.VMEM((Q_BLOCK + 2 * HALF, D_MODEL), BF16)],
        compiler_params=_cparams(("parallel", "parallel")),
    )(qkv, qkv, qkv, qkv, qkv, qkv, qkv)


def _post(x_new, fg_ref, rw_ref, rb_ref, xn_ref, h2_ref, lg_ref):
    xn_ref[...] = x_new
    h2 = _rms(x_new, fg_ref[...])
    h2_ref[...] = h2
    lg_ref[...] = jnp.dot(h2.astype(BF16), rw_ref[...], preferred_element_type=F32) + rb_ref[...]


def _oproj_kernel(o0, o1, o2, l0, l1, l2, wo_ref, x_ref, fg_ref, rw_ref, rb_ref,
                  xn_ref, h2_ref, lg_ref, onat, lnat, mbuf, *, tm):
    o_refs = (o0, o1, o2)
    l_refs = (l0, l1, l2)
    for g, d in enumerate(DILATIONS):
        n = tm // d
        for r in range(d):
            if d == 1:
                lnat[g] = l_refs[g][r]
                for j in range(D_MODEL // 128):
                    onat[g, j] = o_refs[g][r, :, 128 * j:128 * (j + 1)].astype(F32)
            else:
                lnat[g, pl.ds(r, n, stride=d), :] = l_refs[g][r]
                for j in range(D_MODEL // 128):
                    onat[g, j, pl.ds(r, n, stride=d), :] = o_refs[g][r, :, 128 * j:128 * (j + 1)].astype(F32)
    ls = [lnat[g] for g in range(3)]
    m = jnp.maximum(jnp.maximum(ls[0], ls[1]), ls[2])
    es = [jnp.exp(l - m) for l in ls]
    tot = es[0] + es[1] + es[2]
    al = [e / tot for e in es]
    lane = lax.broadcasted_iota(jnp.int32, (tm, 128), 1)
    first = lane < HEAD_DIM
    for j in range(D_MODEL // 128):
        acc = None
        for g in range(3):
            a = jnp.where(first, al[g][:, 2 * j:2 * j + 1], al[g][:, 2 * j + 1:2 * j + 2])
            t = a * onat[g, j]
            acc = t if acc is None else acc + t
        mbuf[:, 128 * j:128 * (j + 1)] = acc.astype(BF16)
    y = jnp.dot(mbuf[...], wo_ref[...], preferred_element_type=F32)
    _post(x_ref[...] + y, fg_ref, rw_ref, rb_ref, xn_ref, h2_ref, lg_ref)


def _oproj_call(os_, ls_, wo, x, fg, rw, rb, *, B, S, tm=256):
    T = B * S
    nI = S // tm
    in_specs = []
    for d in DILATIONS:
        in_specs.append(pl.BlockSpec((None, d, tm // d, D_MODEL), lambda i: (i // nI, 0, i % nI, 0)))
    for d in DILATIONS:
        in_specs.append(pl.BlockSpec((None, d, tm // d, 128), lambda i: (i // nI, 0, i % nI, 0)))
    row = pl.BlockSpec((tm, D_MODEL), lambda i: (i, 0))
    in_specs += [
        pl.BlockSpec((D_MODEL, D_MODEL), lambda i: (0, 0)),
        row,
        pl.BlockSpec((1, D_MODEL), lambda i: (0, 0)),
        pl.BlockSpec((D_MODEL, ROUTER_LANES), lambda i: (0, 0)),
        pl.BlockSpec((1, ROUTER_LANES), lambda i: (0, 0)),
    ]
    return pl.pallas_call(
        functools.partial(_oproj_kernel, tm=tm),
        grid=(T // tm,),
        in_specs=in_specs,
        out_specs=[row, row, pl.BlockSpec((tm, ROUTER_LANES), lambda i: (i, 0))],
        out_shape=[jax.ShapeDtypeStruct((T, D_MODEL), F32), jax.ShapeDtypeStruct((T, D_MODEL), F32),
                   jax.ShapeDtypeStruct((T, ROUTER_LANES), F32)],
        scratch_shapes=[pltpu.VMEM((3, D_MODEL // 128, tm, 128), F32), pltpu.VMEM((3, tm, 128), F32),
                        pltpu.VMEM((tm, D_MODEL), BF16)],
        compiler_params=_cparams(("parallel",)),
    )(*os_, *ls_, wo, x, fg, rw, rb)


def _pool_kernel(x_ref, ya_ref, yb_ref, xp_ref, yap_ref, ybp_ref, xq_ref, yaq_ref, ybq_ref,
                 g_ref, wp_ref, ps_ref, fg_ref, rw_ref, rb_ref, xn_ref, h2_ref, lg_ref, hp, xnew, *, tm, S):
    i = pl.program_id(0)
    nI = S // tm
    ii = i % nI
    g = g_ref[...]
    xs = x_ref[...] + ya_ref[...] + yb_ref[...]
    h = _rms(xs, g)
    hprev = _rms(xp_ref[...] + yap_ref[...] + ybp_ref[...], g)
    hnext = _rms(xq_ref[...] + yaq_ref[...] + ybq_ref[...], g)
    hp[0:HALO, :] = jnp.where(ii > 0, hprev, 0.0)
    hp[HALO:HALO + tm, :] = h
    hp[HALO + tm:2 * HALO + tm, :] = jnp.where(ii < nI - 1, hnext, 0.0)
    t = ii * tm + lax.broadcasted_iota(jnp.int32, (tm, 1), 0)
    for k, w in enumerate(POOL_WINDOWS):
        cols = slice(k * POOL_CH, (k + 1) * POOL_CH)
        acc = None
        for j in range(-(w // 2), w // 2):
            v = hp[HALO + j:HALO + j + tm, cols]
            acc = v if acc is None else acc + v
        lo = jnp.clip(t - w // 2, 0, S)
        hi = jnp.clip(t - w // 2 + w, 0, S)
        mean = acc / (hi - lo).astype(F32)
        diff = (mean - h[:, cols]).astype(BF16)
        out = jnp.dot(diff, wp_ref[k], preferred_element_type=F32)
        xnew[:, cols] = xs[:, cols] + out * ps_ref[:, cols]
    _post(xnew[...], fg_ref, rw_ref, rb_ref, xn_ref, h2_ref, lg_ref)


def _pool_call(x, y2, g, wp, ps, fg, rw, rb, *, B, S, tm=512):
    T = B * S
    nb8 = T // HALO
    r8 = tm // HALO
    nT = T // tm

    def main(k):
        return pl.BlockSpec((tm, D_MODEL), lambda i: (k * nT + i, 0))

    def prev(k):
        return pl.BlockSpec((HALO, D_MODEL), lambda i: (k * nb8 + jnp.maximum(i * r8 - 1, 0), 0))

    def nxt(k):
        return pl.BlockSpec((HALO, D_MODEL), lambda i: (k * nb8 + jnp.minimum((i + 1) * r8, nb8 - 1), 0))

    row = pl.BlockSpec((tm, D_MODEL), lambda i: (i, 0))
    vec = pl.BlockSpec((1, D_MODEL), lambda i: (0, 0))
    in_specs = [main(0), main(0), main(1), prev(0), prev(0), prev(1), nxt(0), nxt(0), nxt(1),
                vec, pl.BlockSpec((len(POOL_WINDOWS), POOL_CH, POOL_CH), lambda i: (0, 0, 0)), vec, vec,
                pl.BlockSpec((D_MODEL, ROUTER_LANES), lambda i: (0, 0)),
                pl.BlockSpec((1, ROUTER_LANES), lambda i: (0, 0))]
    return pl.pallas_call(
        functools.partial(_pool_kernel, tm=tm, S=S),
        grid=(nT,),
        in_specs=in_specs,
        out_specs=[row, row, pl.BlockSpec((tm, ROUTER_LANES), lambda i: (i, 0))],
        out_shape=[jax.ShapeDtypeStruct((T, D_MODEL), F32), jax.ShapeDtypeStruct((T, D_MODEL), F32),
                   jax.ShapeDtypeStruct((T, ROUTER_LANES), F32)],
        scratch_shapes=[pltpu.VMEM((tm + 2 * HALO, D_MODEL), F32), pltpu.VMEM((tm, D_MODEL), F32)],
        compiler_params=_cparams(("parallel",)),
    )(x, y2, y2, x, y2, y2, x, y2, y2, g, wp, ps, fg, rw, rb)


def _moe_kernel(be_ref, tokc_ref, tokn_ref, dst_ref, gate_ref, h_hbm, wg_ref, wu_ref, wd_ref, y_hbm,
                xbuf, ybuf, gsem, ssem):
    del be_ref
    i = pl.program_id(0)
    nb = pl.num_programs(0)
    slot = i % 2

    def gather(tok_ref, s):
        for r in range(MOE_BLOCK):
            pltpu.make_async_copy(h_hbm.at[pl.ds(tok_ref[0, r], 1)], xbuf.at[s, pl.ds(r, 1)], gsem.at[s]).start()

    def gather_wait(s):
        for r in range(MOE_BLOCK):
            pltpu.make_async_copy(h_hbm.at[pl.ds(0, 1)], xbuf.at[s, pl.ds(r, 1)], gsem.at[s]).wait()

    def scatter_wait(s):
        for r in range(MOE_BLOCK):
            pltpu.make_async_copy(ybuf.at[s, pl.ds(r, 1)], y_hbm.at[pl.ds(0, 1)], ssem.at[s]).wait()

    @pl.when(i == 0)
    def _():
        gather(tokc_ref, 0)

    @pl.when(i + 1 < nb)
    def _():
        gather(tokn_ref, 1 - slot)

    gather_wait(slot)

    @pl.when(i >= 2)
    def _():
        scatter_wait(slot)

    x = xbuf[slot].astype(BF16)
    g = jnp.dot(x, wg_ref[...], preferred_element_type=F32)
    u = jnp.dot(x, wu_ref[...], preferred_element_type=F32)
    a = (g * (1.0 / (1.0 + jnp.exp(-g)))) * u
    y = jnp.dot(a.astype(BF16), wd_ref[...], preferred_element_type=F32)
    ybuf[slot] = y * gate_ref[...]
    for r in range(MOE_BLOCK):
        pltpu.make_async_copy(ybuf.at[slot, pl.ds(r, 1)], y_hbm.at[pl.ds(dst_ref[0, r], 1)], ssem.at[slot]).start()

    @pl.when(i == nb - 1)
    def _():
        scatter_wait(slot)

        @pl.when(nb >= 2)
        def _():
            scatter_wait(1 - slot)


def _moe_call(h2, block_e, tok, dst, gate, wg, wu, wd, *, T):
    nb = block_e.shape[0]
    smem_row = functools.partial(pl.BlockSpec, (None, 1, MOE_BLOCK), memory_space=pltpu.SMEM)
    grid_spec = pltpu.PrefetchScalarGridSpec(
        num_scalar_prefetch=1,
        grid=(nb,),
        in_specs=[
            smem_row(lambda i, be: (i, 0, 0)),
            smem_row(lambda i, be: (jnp.minimum(i + 1, nb - 1), 0, 0)),
            smem_row(lambda i, be: (i, 0, 0)),
            pl.BlockSpec((MOE_BLOCK, 1), lambda i, be: (i, 0)),
            pl.BlockSpec(memory_space=pl.ANY),
            pl.BlockSpec((None, D_MODEL, D_EXPERT), lambda i, be: (be[i], 0, 0)),
            pl.BlockSpec((None, D_MODEL, D_EXPERT), lambda i, be: (be[i], 0, 0)),
            pl.BlockSpec((None, D_EXPERT, D_MODEL), lambda i, be: (be[i], 0, 0)),
        ],
        out_specs=pl.BlockSpec(memory_space=pl.ANY),
        scratch_shapes=[pltpu.VMEM((2, MOE_BLOCK, D_MODEL), F32), pltpu.VMEM((2, MOE_BLOCK, D_MODEL), F32),
                        pltpu.SemaphoreType.DMA((2,)), pltpu.SemaphoreType.DMA((2,))],
    )
    return pl.pallas_call(
        _moe_kernel,
        grid_spec=grid_spec,
        out_shape=jax.ShapeDtypeStruct((2 * T + 2 * MOE_BLOCK, D_MODEL), F32),
        compiler_params=_cparams(("arbitrary",)),
    )(block_e, tok, tok, dst, gate, h2, wg, wu, wd)


def _final_kernel(x_ref, ya_ref, yb_ref, g_ref, o_ref):
    o_ref[...] = _rms(x_ref[...] + ya_ref[...] + yb_ref[...], g_ref[...])


def _final_call(x, y2, g, *, T, tm=512):
    nT = T // tm
    return pl.pallas_call(
        _final_kernel,
        grid=(nT,),
        in_specs=[pl.BlockSpec((tm, D_MODEL), lambda i: (i, 0)),
                  pl.BlockSpec((tm, D_MODEL), lambda i: (i, 0)),
                  pl.BlockSpec((tm, D_MODEL), lambda i: (nT + i, 0)),
                  pl.BlockSpec((1, D_MODEL), lambda i: (0, 0))],
        out_specs=pl.BlockSpec((tm, D_MODEL), lambda i: (i, 0)),
        out_shape=jax.ShapeDtypeStruct((T, D_MODEL), F32),
        compiler_params=_cparams(("parallel",)),
    )(x, y2, y2, g)


def _route(logits, T):
    lg = logits[:, :N_GROUPS]
    le = logits[:, N_GROUPS:N_GROUPS + N_EXPERTS].reshape(T, N_GROUPS, EPG)
    gprob = jax.nn.softmax(lg, axis=-1)
    gp, gi = lax.top_k(gprob, 1)
    gp, gi = gp[:, 0], gi[:, 0]
    gidx = jnp.broadcast_to(gi[:, None, None], (T, 1, EPG))
    eprob = jax.nn.softmax(jnp.take_along_axis(le, gidx, axis=1)[:, 0], axis=-1)
    ep, ei = lax.top_k(eprob, TOP_K)
    gates = gp[:, None] * ep / jnp.sum(ep, axis=-1, keepdims=True)
    expert_id = gi[:, None] * EPG + ei

    A = T * TOP_K
    e_flat = expert_id.reshape(A).astype(jnp.int32)
    g_flat = gates.reshape(A)
    order = jnp.argsort(e_flat, stable=True).astype(jnp.int32)
    counts = jnp.sum(e_flat[:, None] == jnp.arange(N_EXPERTS, dtype=jnp.int32)[None, :], axis=0, dtype=jnp.int32)
    starts = jnp.cumsum(counts) - counts
    pcounts = (counts + MOE_BLOCK - 1) // MOE_BLOCK * MOE_BLOCK
    pends = jnp.cumsum(pcounts)
    pstarts = pends - pcounts
    nb = A // MOE_BLOCK + N_EXPERTS
    block_e = jnp.clip(jnp.searchsorted(pends, jnp.arange(nb, dtype=jnp.int32) * MOE_BLOCK, side='right'),
                       0, N_EXPERTS - 1).astype(jnp.int32)
    slot = jnp.arange(nb * MOE_BLOCK, dtype=jnp.int32)
    blk = slot // MOE_BLOCK
    be = block_e[blk]
    off = slot - pstarts[be]
    valid = (off >= 0) & (off < counts[be])
    a_idx = order[jnp.clip(starts[be] + off, 0, A - 1)]
    tok = jnp.where(valid, a_idx // TOP_K, 0)
    junk = 2 * T + (blk % 2) * MOE_BLOCK + slot % MOE_BLOCK
    dst = jnp.where(valid, (a_idx % TOP_K) * T + a_idx // TOP_K, junk)
    gate = jnp.where(valid, g_flat[a_idx], 0.0)
    return (block_e, tok.reshape(nb, 1, MOE_BLOCK), dst.reshape(nb, 1, MOE_BLOCK),
            gate.reshape(nb * MOE_BLOCK, 1))


def _rope_tables(S, d, tm):
    n = tm // d
    j = jnp.arange(S, dtype=jnp.int32)
    tile, rem = j // tm, j % tm
    pos = ((tile * n + rem % n) * d + rem // n).astype(F32)
    inv_freq = 1.0 / (ROPE_THETA ** (jnp.arange(0, HEAD_DIM, 2, dtype=F32) / HEAD_DIM))
    ang = pos[:, None] * inv_freq[None, :]
    cos, sin = jnp.cos(ang), jnp.sin(ang)
    return jnp.tile(cos, (1, 4)), jnp.concatenate([-sin, sin, -sin, sin], axis=1)


def _trunk(x, p, *, B, S):
    T = B * S
    x = x.reshape(T, D_MODEL)
    tm_qkv = 512
    tables = [_rope_tables(S, d, tm_qkv) for d in DILATIONS]
    y2 = None
    for i in range(DEPTH):
        gm = p["mixer_norm"][i][None, :]
        fg = p["ffn_norm"][i][None, :]
        rw, rb = p["router_w"][i], p["router_b"][i]
        if i % 2 == 0:
            li = i // 2
            os_, ls_ = [], []
            adds = [] if y2 is None else [y2, y2]
            xs = x
            for gi_, d in enumerate(DILATIONS):
                emit = bool(adds) and gi_ == 0
                res = _qkv_call(x, adds, gm, p["w_qkv"][li][gi_], tables[gi_][0], tables[gi_][1],
                                B=B, S=S, d=d, emit_x=emit, tm=tm_qkv)
                if emit:
                    xs, qkv = res
                else:
                    (qkv,) = res
                L = S // d
                o, lse = _attn_call(qkv.reshape(B * d, L, 3 * D_MODEL), L=L)
                os_.append(o.reshape(B, d, L, D_MODEL))
                ls_.append(lse.reshape(B, d, L, 128))
            x, h2, logits = _oproj_call(os_, ls_, p["w_o"][li], xs, fg, rw, rb, B=B, S=S)
        else:
            li = i // 2
            x, h2, logits = _pool_call(x, y2, gm, p["w_pool"][li], p["pool_scale"][li][None, :], fg, rw, rb,
                                       B=B, S=S)
        block_e, tok, dst, gate = _route(logits, T)
        y2 = _moe_call(h2, block_e, tok, dst, gate, p["w_gate"][i], p["w_up"][i], p["w_down"][i], T=T)
    out = _final_call(x, y2, p["final_norm"][None, :], T=T)
    return out.reshape(B, S, D_MODEL)


def kernel(x_prompt, x_sample, mixer_norm, w_qkv, w_o, w_pool, pool_scale, ffn_norm, router_group_w,
           router_group_b, router_expert_w, router_expert_b, w_gate, w_up, w_down, final_norm):
    n_attn = w_qkv.shape[0]
    wq = w_qkv.astype(BF16).reshape(n_attn, D_MODEL, len(DILATIONS), 3 * D_MODEL).transpose(0, 2, 1, 3)
    pad = ROUTER_LANES - N_GROUPS - N_EXPERTS
    router_w = jnp.pad(jnp.concatenate([router_group_w, router_expert_w], axis=-1),
                       ((0, 0), (0, 0), (0, pad))).astype(BF16)
    router_b = jnp.pad(jnp.concatenate([router_group_b, router_expert_b], axis=-1), ((0, 0), (0, pad)))[:, None, :]
    p = dict(mixer_norm=mixer_norm, ffn_norm=ffn_norm, final_norm=final_norm, pool_scale=pool_scale,
             w_qkv=wq, w_o=w_o.astype(BF16), w_pool=w_pool.astype(BF16),
             router_w=router_w, router_b=router_b,
             w_gate=w_gate.astype(BF16), w_up=w_up.astype(BF16), w_down=w_down.astype(BF16))
    y_prompt = _trunk(x_prompt, p, B=x_prompt.shape[0], S=x_prompt.shape[1])
    y_sample = _trunk(x_sample, p, B=x_sample.shape[0], S=x_sample.shape[1])
    return (y_prompt, y_sample)
```

```python
import functools

import jax
import jax.numpy as jnp
from jax import lax
from jax.experimental import pallas as pl
from jax.experimental.pallas import tpu as pltpu

F32 = jnp.float32
BF16 = jnp.bfloat16

D_MODEL = 1024
DEPTH = 4
DILATIONS = (1, 4, 16)
HALF = 64
N_HEADS = 16
HEAD_DIM = 64
Q_BLOCK = 128
ROPE_THETA = 10000.0
NEG_INF = -1e30
POOL_WINDOWS = (2, 4, 8, 16)
POOL_CH = 256
N_GROUPS = 4
EPG = 8
N_EXPERTS = 32
TOP_K = 2
D_EXPERT = 512
MOE_TILE = 256
RMS_EPS = 1e-6
ROUTER_LANES = 128
HALO = 8
VMEM_LIMIT = 48 * 1024 * 1024


def _rms(x, g):
    return x * lax.rsqrt(jnp.mean(x * x, axis=-1, keepdims=True) + RMS_EPS) * g


def _cparams(sem):
    return pltpu.CompilerParams(dimension_semantics=sem, vmem_limit_bytes=VMEM_LIMIT)


LANE_CHUNKS = D_MODEL // 128


def _read_tile_rows(ref, n, start=0):
    return jnp.concatenate([ref[pl.ds(start * LANE_CHUNKS + j, n, stride=LANE_CHUNKS), :]
                            for j in range(LANE_CHUNKS)], axis=1)


def _write_tile_rows(ref, val, start=0):
    n = val.shape[0]
    for j in range(LANE_CHUNKS):
        ref[pl.ds(start * LANE_CHUNKS + j, n, stride=LANE_CHUNKS), :] = val[:, 128 * j:128 * (j + 1)]


def _qkv_kernel(*refs, n_add, d, tm, emit_x):
    x_ref = refs[0]
    add_refs = refs[1:1 + n_add]
    g_ref, w_ref, cos_ref, sin_ref = refs[1 + n_add:5 + n_add]
    outs = refs[5 + n_add:]
    if emit_x:
        xs_ref, out_ref, hperm, hf = outs
    else:
        out_ref, hperm, hf = outs
    c = pl.program_id(1)
    n = tm // d

    @pl.when(c == 0)
    def _():
        xs = x_ref[...]
        for a in add_refs:
            xs = xs + _read_tile_rows(a, tm)
        if emit_x:
            xs_ref[...] = xs
        h = _rms(xs, g_ref[...])
        if d == 1:
            hperm[...] = h.astype(BF16)
        else:
            for j in range(D_MODEL // 128):
                hf[j] = h[:, 128 * j:128 * (j + 1)]
            for r in range(d):
                for j in range(D_MODEL // 128):
                    hperm[r * n:(r + 1) * n, 128 * j:128 * (j + 1)] = hf[j, pl.ds(r, n, stride=d), :].astype(BF16)

    CW = 256

    def project(j):
        return jnp.dot(hperm[...], w_ref[:, CW * j:CW * (j + 1)], preferred_element_type=F32)

    def store(col, res):
        for r in range(d):
            out_ref[r, :, col:col + res.shape[1]] = res[r * n:(r + 1) * n, :]

    def rope(scale):
        lane = lax.broadcasted_iota(jnp.int32, (tm, 128), 1)
        low = (lane % HEAD_DIM) < (HEAD_DIM // 2)
        cs = cos_ref[...]
        sn = sin_ref[...]
        for j in range(D_MODEL // CW):
            acc = project(j)
            for k in range(CW // 128):
                xj = acc[:, 128 * k:128 * (k + 1)]
                partner = jnp.where(low, pltpu.roll(xj, 128 - HEAD_DIM // 2, 1), pltpu.roll(xj, HEAD_DIM // 2, 1))
                o = xj * cs + partner * sn
                if scale != 1.0:
                    o = o * scale
                store(CW * j + 128 * k, o.astype(BF16))

    @pl.when(c == 0)
    def _():
        rope(HEAD_DIM ** -0.5)

    @pl.when(c == 1)
    def _():
        rope(1.0)

    @pl.when(c == 2)
    def _():
        for j in range(D_MODEL // CW):
            store(CW * j, project(j).astype(BF16))


def _qkv_call(x, adds, g, w, cos, sin, *, B, S, d, emit_x, tm=512):
    T = B * S
    nI = S // tm
    L = S // d
    n = tm // d
    n_add = len(adds)
    row = pl.BlockSpec((tm, D_MODEL), lambda i, c: (i, 0))
    in_specs = [row]
    T_rows = T // tm
    for k in range(n_add):
        in_specs.append(pl.BlockSpec((tm * LANE_CHUNKS, 128), lambda i, c, k=k: (k * T_rows + i, 0)))
    in_specs += [
        pl.BlockSpec((1, D_MODEL), lambda i, c: (0, 0)),
        pl.BlockSpec((D_MODEL, D_MODEL), lambda i, c: (0, c)),
        pl.BlockSpec((tm, 128), lambda i, c: (i % nI, 0)),
        pl.BlockSpec((tm, 128), lambda i, c: (i % nI, 0)),
    ]
    out_spec = pl.BlockSpec((None, d, n, D_MODEL), lambda i, c: (i // nI, 0, i % nI, c))
    out_shape = jax.ShapeDtypeStruct((B, d, L, 3 * D_MODEL), BF16)
    if emit_x:
        out_specs = [row, out_spec]
        out_shapes = [jax.ShapeDtypeStruct((T, D_MODEL), F32), out_shape]
    else:
        out_specs = [out_spec]
        out_shapes = [out_shape]
    res = pl.pallas_call(
        functools.partial(_qkv_kernel, n_add=n_add, d=d, tm=tm, emit_x=emit_x),
        grid=(T // tm, 3),
        in_specs=in_specs,
        out_specs=out_specs,
        out_shape=out_shapes,
        scratch_shapes=[pltpu.VMEM((tm, D_MODEL), BF16), pltpu.VMEM((D_MODEL // 128, tm, 128), F32)],
        compiler_params=_cparams(("parallel", "arbitrary")),
        name=f"qkv_d{d}",
    )(x, *adds, g, w, cos, sin)
    return res


def _attn_kernel(q_ref, kp_ref, kc_ref, kn_ref, vp_ref, vc_ref, vn_ref, o_ref, lse_ref, kbuf, vbuf, *, L):
    i = pl.program_id(1)
    W = Q_BLOCK + 2 * HALF
    kbuf[0:HALF, :] = kp_ref[...]
    kbuf[HALF:HALF + Q_BLOCK, :] = kc_ref[...]
    kbuf[HALF + Q_BLOCK:W, :] = kn_ref[...]
    vbuf[0:HALF, :] = vp_ref[...]
    vbuf[HALF:HALF + Q_BLOCK, :] = vc_ref[...]
    vbuf[HALF + Q_BLOCK:W, :] = vn_ref[...]
    row = lax.broadcasted_iota(jnp.int32, (Q_BLOCK, W), 0)
    col = lax.broadcasted_iota(jnp.int32, (Q_BLOCK, W), 1)
    rel = col - row
    kpos = i * Q_BLOCK - HALF + col
    mask = (rel >= 0) & (rel <= 2 * HALF) & (kpos >= 0) & (kpos < L)
    lane = lax.broadcasted_iota(jnp.int32, (Q_BLOCK, 128), 1)
    lse_tile = jnp.zeros((Q_BLOCK, 128), F32)
    for h in range(N_HEADS):
        sl = slice(h * HEAD_DIM, (h + 1) * HEAD_DIM)
        s = lax.dot_general(q_ref[:, sl], kbuf[:, sl], (((1,), (1,)), ((), ())), preferred_element_type=F32)
        s = jnp.where(mask, s, NEG_INF)
        m = jnp.max(s, axis=-1, keepdims=True)
        p = jnp.exp(s - m)
        l = jnp.sum(p, axis=-1, keepdims=True)
        o = jnp.dot(p.astype(BF16), vbuf[:, sl], preferred_element_type=F32) / l
        o_ref[:, sl] = o.astype(BF16)
        lse_tile = jnp.where(lane == h, m + jnp.log(l), lse_tile)
    lse_ref[...] = lse_tile


def _attn_call(qkv, *, L):
    NS = qkv.shape[0]
    nq = L // Q_BLOCK
    nh = L // HALF
    q_spec = pl.BlockSpec((None, Q_BLOCK, D_MODEL), lambda s, i: (s, i, 0))

    def cur(cb):
        return pl.BlockSpec((None, Q_BLOCK, D_MODEL), lambda s, i: (s, i, cb))

    def prev(cb):
        return pl.BlockSpec((None, HALF, D_MODEL), lambda s, i: (s, jnp.maximum(2 * i - 1, 0), cb))

    def nxt(cb):
        return pl.BlockSpec((None, HALF, D_MODEL), lambda s, i: (s, jnp.minimum(2 * i + 2, nh - 1), cb))

    return pl.pallas_call(
        functools.partial(_attn_kernel, L=L),
        grid=(NS, nq),
        in_specs=[q_spec, prev(1), cur(1), nxt(1), prev(2), cur(2), nxt(2)],
        out_specs=[pl.BlockSpec((None, Q_BLOCK, D_MODEL), lambda s, i: (s, i, 0)),
                   pl.BlockSpec((None, Q_BLOCK, 128), lambda s, i: (s, i, 0))],
        out_shape=[jax.ShapeDtypeStruct((NS, L, D_MODEL), BF16),
                   jax.ShapeDtypeStruct((NS, L, 128), F32)],
        scratch_shapes=[pltpu.VMEM((Q_BLOCK + 2 * HALF, D_MODEL), BF16),
                        pltpu.VMEM((Q_BLOCK + 2 * HALF, D_MODEL), BF16)],
        compiler_params=_cparams(("parallel", "parallel")),
        name="band_attn",
    )(qkv, qkv, qkv, qkv, qkv, qkv, qkv)


def _post(x_new, fg_ref, rw_ref, rb_ref, xn_ref, h2_ref, rt_ref):
    xn_ref[...] = x_new
    h2 = _rms(x_new, fg_ref[...])
    _write_tile_rows(h2_ref, h2)
    lg = jnp.dot(h2.astype(BF16), rw_ref[...], preferred_element_type=F32) + rb_ref[...]
    lane = lax.broadcasted_iota(jnp.int32, lg.shape, 1)

    def first_argmax(v, vmax):
        return jnp.min(jnp.where(v == vmax, lane, ROUTER_LANES), axis=-1, keepdims=True)

    is_g = lane < N_GROUPS
    vg = jnp.where(is_g, lg, -jnp.inf)
    gmax = jnp.max(vg, axis=-1, keepdims=True)
    gi = first_argmax(vg, gmax)
    gp = 1.0 / jnp.sum(jnp.where(is_g, jnp.exp(lg - gmax), 0.0), axis=-1, keepdims=True)
    lo = N_GROUPS + EPG * gi
    in_grp = (lane >= lo) & (lane < lo + EPG)
    ve = jnp.where(in_grp, lg, -jnp.inf)
    m1 = jnp.max(ve, axis=-1, keepdims=True)
    i1 = first_argmax(ve, m1)
    ve2 = jnp.where(lane == i1, -jnp.inf, ve)
    m2 = jnp.max(ve2, axis=-1, keepdims=True)
    i2 = first_argmax(ve2, m2)
    esum = jnp.sum(jnp.where(in_grp, jnp.exp(lg - m1), 0.0), axis=-1, keepdims=True)
    ep1 = 1.0 / esum
    ep2 = jnp.exp(m2 - m1) / esum
    den = ep1 + ep2
    g1 = gp * ep1 / den
    g2 = gp * ep2 / den
    e1 = (i1 - N_GROUPS).astype(F32)
    e2 = (i2 - N_GROUPS).astype(F32)
    rt_ref[...] = jnp.where(lane == 0, e1, jnp.where(lane == 1, e2, jnp.where(lane == 2, g1,
                            jnp.where(lane == 3, g2, 0.0))))


def _oproj_kernel(o0, o1, o2, l0, l1, l2, wo_ref, x_ref, fg_ref, rw_ref, rb_ref,
                  xn_ref, h2_ref, lg_ref, onat, lnat, mbuf, *, tm):
    o_refs = (o0, o1, o2)
    l_refs = (l0, l1, l2)
    for g, d in enumerate(DILATIONS):
        n = tm // d
        for r in range(d):
            if d == 1:
                lnat[g] = l_refs[g][r]
                for j in range(D_MODEL // 128):
                    onat[g, j] = o_refs[g][r, :, 128 * j:128 * (j + 1)].astype(F32)
            else:
                lnat[g, pl.ds(r, n, stride=d), :] = l_refs[g][r]
                for j in range(D_MODEL // 128):
                    onat[g, j, pl.ds(r, n, stride=d), :] = o_refs[g][r, :, 128 * j:128 * (j + 1)].astype(F32)
    ls = [lnat[g] for g in range(3)]
    m = jnp.maximum(jnp.maximum(ls[0], ls[1]), ls[2])
    es = [jnp.exp(l - m) for l in ls]
    tot = es[0] + es[1] + es[2]
    al = [e / tot for e in es]
    lane = lax.broadcasted_iota(jnp.int32, (tm, 128), 1)
    first = lane < HEAD_DIM
    for j in range(D_MODEL // 128):
        acc = None
        for g in range(3):
            a = jnp.where(first, al[g][:, 2 * j:2 * j + 1], al[g][:, 2 * j + 1:2 * j + 2])
            t = a * onat[g, j]
            acc = t if acc is None else acc + t
        mbuf[:, 128 * j:128 * (j + 1)] = acc.astype(BF16)
    y = jnp.dot(mbuf[...], wo_ref[...], preferred_element_type=F32)
    _post(x_ref[...] + y, fg_ref, rw_ref, rb_ref, xn_ref, h2_ref, lg_ref)


def _oproj_call(os_, ls_, wo, x, fg, rw, rb, *, B, S, tm=256):
    T = B * S
    nI = S // tm
    in_specs = []
    for d in DILATIONS:
        in_specs.append(pl.BlockSpec((None, d, tm // d, D_MODEL), lambda i: (i // nI, 0, i % nI, 0)))
    for d in DILATIONS:
        in_specs.append(pl.BlockSpec((None, d, tm // d, 128), lambda i: (i // nI, 0, i % nI, 0)))
    row = pl.BlockSpec((tm, D_MODEL), lambda i: (i, 0))
    in_specs += [
        pl.BlockSpec((D_MODEL, D_MODEL), lambda i: (0, 0)),
        row,
        pl.BlockSpec((1, D_MODEL), lambda i: (0, 0)),
        pl.BlockSpec((D_MODEL, ROUTER_LANES), lambda i: (0, 0)),
        pl.BlockSpec((1, ROUTER_LANES), lambda i: (0, 0)),
    ]
    return pl.pallas_call(
        functools.partial(_oproj_kernel, tm=tm),
        grid=(T // tm,),
        in_specs=in_specs,
        out_specs=[row, pl.BlockSpec((tm * LANE_CHUNKS, 128), lambda i: (i, 0)),
                   pl.BlockSpec((tm, ROUTER_LANES), lambda i: (i, 0))],
        out_shape=[jax.ShapeDtypeStruct((T, D_MODEL), F32), jax.ShapeDtypeStruct((T * LANE_CHUNKS, 128), F32),
                   jax.ShapeDtypeStruct((T, ROUTER_LANES), F32)],
        scratch_shapes=[pltpu.VMEM((3, D_MODEL // 128, tm, 128), F32), pltpu.VMEM((3, tm, 128), F32),
                        pltpu.VMEM((tm, D_MODEL), BF16)],
        compiler_params=_cparams(("parallel",)),
        name="merge_oproj",
    )(*os_, *ls_, wo, x, fg, rw, rb)


def _pool_kernel(x_ref, ya_ref, yb_ref, xp_ref, yap_ref, ybp_ref, xq_ref, yaq_ref, ybq_ref,
                 g_ref, wp_ref, ps_ref, fg_ref, rw_ref, rb_ref, xn_ref, h2_ref, lg_ref, hp, xnew, *, tm, S):
    i = pl.program_id(0)
    nI = S // tm
    ii = i % nI
    g = g_ref[...]
    xs = x_ref[...] + _read_tile_rows(ya_ref, tm) + _read_tile_rows(yb_ref, tm)
    h = _rms(xs, g)
    hprev = _rms(xp_ref[...] + _read_tile_rows(yap_ref, HALO) + _read_tile_rows(ybp_ref, HALO), g)
    hnext = _rms(xq_ref[...] + _read_tile_rows(yaq_ref, HALO) + _read_tile_rows(ybq_ref, HALO), g)
    hp[0:HALO, :] = jnp.where(ii > 0, hprev, 0.0)
    hp[HALO:HALO + tm, :] = h
    hp[HALO + tm:2 * HALO + tm, :] = jnp.where(ii < nI - 1, hnext, 0.0)
    t = ii * tm + lax.broadcasted_iota(jnp.int32, (tm, 1), 0)
    for k, w in enumerate(POOL_WINDOWS):
        cols = slice(k * POOL_CH, (k + 1) * POOL_CH)
        acc = None
        for j in range(-(w // 2), w // 2):
            v = hp[HALO + j:HALO + j + tm, cols]
            acc = v if acc is None else acc + v
        lo = jnp.clip(t - w // 2, 0, S)
        hi = jnp.clip(t - w // 2 + w, 0, S)
        mean = acc / (hi - lo).astype(F32)
        diff = (mean - h[:, cols]).astype(BF16)
        out = jnp.dot(diff, wp_ref[k], preferred_element_type=F32)
        xnew[:, cols] = xs[:, cols] + out * ps_ref[:, cols]
    _post(xnew[...], fg_ref, rw_ref, rb_ref, xn_ref, h2_ref, lg_ref)


def _pool_call(x, y2, g, wp, ps, fg, rw, rb, *, B, S, tm=512):
    T = B * S
    nb8 = T // HALO
    r8 = tm // HALO
    nT = T // tm

    def main(k, shape=(tm, D_MODEL)):
        return pl.BlockSpec(shape, lambda i: (k * nT + i, 0))

    def prev(k, shape=(HALO, D_MODEL)):
        return pl.BlockSpec(shape, lambda i: (k * nb8 + jnp.maximum(i * r8 - 1, 0), 0))

    def nxt(k, shape=(HALO, D_MODEL)):
        return pl.BlockSpec(shape, lambda i: (k * nb8 + jnp.minimum((i + 1) * r8, nb8 - 1), 0))

    row = pl.BlockSpec((tm, D_MODEL), lambda i: (i, 0))
    vec = pl.BlockSpec((1, D_MODEL), lambda i: (0, 0))
    tr_main = (tm * LANE_CHUNKS, 128)
    tr_halo = (HALO * LANE_CHUNKS, 128)
    in_specs = [main(0), main(0, tr_main), main(1, tr_main), prev(0), prev(0, tr_halo), prev(1, tr_halo),
                nxt(0), nxt(0, tr_halo), nxt(1, tr_halo),
                vec, pl.BlockSpec((len(POOL_WINDOWS), POOL_CH, POOL_CH), lambda i: (0, 0, 0)), vec, vec,
                pl.BlockSpec((D_MODEL, ROUTER_LANES), lambda i: (0, 0)),
                pl.BlockSpec((1, ROUTER_LANES), lambda i: (0, 0))]
    return pl.pallas_call(
        functools.partial(_pool_kernel, tm=tm, S=S),
        grid=(nT,),
        in_specs=in_specs,
        out_specs=[row, pl.BlockSpec(tr_main, lambda i: (i, 0)),
                   pl.BlockSpec((tm, ROUTER_LANES), lambda i: (i, 0))],
        out_shape=[jax.ShapeDtypeStruct((T, D_MODEL), F32), jax.ShapeDtypeStruct((T * LANE_CHUNKS, 128), F32),
                   jax.ShapeDtypeStruct((T, ROUTER_LANES), F32)],
        scratch_shapes=[pltpu.VMEM((tm + 2 * HALO, D_MODEL), F32), pltpu.VMEM((tm, D_MODEL), F32)],
        compiler_params=_cparams(("parallel",)),
        name="pool_mixer",
    )(x, y2, y2, x, y2, y2, x, y2, y2, g, wp, ps, fg, rw, rb)


def _moe_kernel(vt_ref, ve_ref, vlo_ref, vhi_ref, tokc_ref, tokn_ref, dst_ref, gate_ref, h_hbm,
                wg_ref, wu_ref, wd_ref, y_hbm, xbuf, ybuf, gsem, ssem, *, T):
    del ve_ref
    i = pl.program_id(0)
    nb = pl.num_programs(0)
    slot = i % 2
    base = vt_ref[i] * MOE_TILE
    lo = vlo_ref[i]
    hi = vhi_ref[i]
    junk = 2 * T + slot * MOE_TILE
    RC = LANE_CHUNKS

    def row_in(tok, s, r):
        return pltpu.make_async_copy(h_hbm.at[pl.ds(pl.multiple_of(tok * RC, RC), RC), :],
                                     xbuf.at[s, pl.ds(r * RC, RC), :], gsem.at[s])

    def row_out(row, s, r):
        return pltpu.make_async_copy(ybuf.at[s, pl.ds(r * RC, RC), :],
                                     y_hbm.at[pl.ds(pl.multiple_of(row * RC, RC), RC), :], ssem.at[s])

    def gather(tok_ref, s):
        for r in range(MOE_TILE):
            row_in(tok_ref[0, r], s, r).start()

    def gather_wait(s):
        for r in range(MOE_TILE):
            row_in(0, s, r).wait()

    def scatter_wait(s):
        for r in range(MOE_TILE):
            row_out(0, s, r).wait()

    @pl.when(i == 0)
    def _():
        gather(tokc_ref, 0)

    @pl.when(i + 1 < nb)
    def _():
        gather(tokn_ref, 1 - slot)

    gather_wait(slot)

    @pl.when(i >= 2)
    def _():
        scatter_wait(slot)

    x = _read_tile_rows(xbuf.at[slot], MOE_TILE).astype(BF16)
    g = jnp.dot(x, wg_ref[...], preferred_element_type=F32)
    u = jnp.dot(x, wu_ref[...], preferred_element_type=F32)
    a = (g * (1.0 / (1.0 + jnp.exp(-g)))) * u
    y = jnp.dot(a.astype(BF16), wd_ref[...], preferred_element_type=F32)
    _write_tile_rows(ybuf.at[slot], y * gate_ref[...])

    whole = (base >= lo) & (base + MOE_TILE <= hi)

    @pl.when(whole)
    def _():
        for r in range(MOE_TILE):
            row_out(dst_ref[0, r], slot, r).start()

    @pl.when(jnp.logical_not(whole))
    def _():
        for r in range(MOE_TILE):
            mine = (base + r >= lo) & (base + r < hi)
            row_out(jnp.where(mine, dst_ref[0, r], junk + r), slot, r).start()

    @pl.when(i == nb - 1)
    def _():
        scatter_wait(slot)

        @pl.when(nb >= 2)
        def _():
            scatter_wait(1 - slot)


def _moe_call(h2, plan, wg, wu, wd, *, T):
    vt, ve, vlo, vhi, tok, dst, gate = plan
    nb = vt.shape[0]
    smem_row = functools.partial(pl.BlockSpec, (None, 1, MOE_TILE), memory_space=pltpu.SMEM)
    buf = pltpu.VMEM((2, MOE_TILE * LANE_CHUNKS, 128), F32)
    grid_spec = pltpu.PrefetchScalarGridSpec(
        num_scalar_prefetch=4,
        grid=(nb,),
        in_specs=[
            smem_row(lambda i, vt, ve, lo, hi: (vt[i], 0, 0)),
            smem_row(lambda i, vt, ve, lo, hi: (vt[jnp.minimum(i + 1, nb - 1)], 0, 0)),
            smem_row(lambda i, vt, ve, lo, hi: (vt[i], 0, 0)),
            pl.BlockSpec((MOE_TILE, 1), lambda i, vt, ve, lo, hi: (vt[i], 0)),
            pl.BlockSpec(memory_space=pl.ANY),
            pl.BlockSpec((None, D_MODEL, D_EXPERT), lambda i, vt, ve, lo, hi: (ve[i], 0, 0)),
            pl.BlockSpec((None, D_MODEL, D_EXPERT), lambda i, vt, ve, lo, hi: (ve[i], 0, 0)),
            pl.BlockSpec((None, D_EXPERT, D_MODEL), lambda i, vt, ve, lo, hi: (ve[i], 0, 0)),
        ],
        out_specs=pl.BlockSpec(memory_space=pl.ANY),
        scratch_shapes=[buf, buf, pltpu.SemaphoreType.DMA((2,)), pltpu.SemaphoreType.DMA((2,))],
    )
    return pl.pallas_call(
        functools.partial(_moe_kernel, T=T),
        grid_spec=grid_spec,
        out_shape=jax.ShapeDtypeStruct(((2 * T + 2 * MOE_TILE) * LANE_CHUNKS, 128), F32),
        compiler_params=_cparams(("arbitrary",)),
        name="moe_experts",
    )(vt, ve, vlo, vhi, tok, tok, dst, gate, h2, wg, wu, wd)


def _final_kernel(x_ref, ya_ref, yb_ref, g_ref, o_ref, *, tm):
    o_ref[...] = _rms(x_ref[...] + _read_tile_rows(ya_ref, tm) + _read_tile_rows(yb_ref, tm), g_ref[...])


def _final_call(x, y2, g, *, T, tm=512):
    nT = T // tm
    return pl.pallas_call(
        functools.partial(_final_kernel, tm=tm),
        grid=(nT,),
        in_specs=[pl.BlockSpec((tm, D_MODEL), lambda i: (i, 0)),
                  pl.BlockSpec((tm * LANE_CHUNKS, 128), lambda i: (i, 0)),
                  pl.BlockSpec((tm * LANE_CHUNKS, 128), lambda i: (nT + i, 0)),
                  pl.BlockSpec((1, D_MODEL), lambda i: (0, 0))],
        out_specs=pl.BlockSpec((tm, D_MODEL), lambda i: (i, 0)),
        out_shape=jax.ShapeDtypeStruct((T, D_MODEL), F32),
        compiler_params=_cparams(("parallel",)),
        name="final_norm",
    )(x, y2, y2, g)


def _plan(route, T):
    A = T * TOP_K
    n_tiles = A // MOE_TILE
    nb = n_tiles + N_EXPERTS - 1
    e_flat = route[:, 0:TOP_K].astype(jnp.int32).reshape(A)
    g_flat = route[:, TOP_K:2 * TOP_K].reshape(A)
    a_iota = jnp.arange(A, dtype=jnp.int32)
    _, a_s, g_s = lax.sort((e_flat, a_iota, g_flat), num_keys=1, is_stable=True)
    tok_s = a_s // TOP_K
    dst_s = (a_s % TOP_K) * T + tok_s
    eids = jnp.arange(N_EXPERTS, dtype=jnp.int32)
    counts = jnp.sum(e_flat[:, None] == eids[None, :], axis=0, dtype=jnp.int32)
    ends = jnp.cumsum(counts)
    starts = ends - counts
    first_tile = starts // MOE_TILE
    nvis = jnp.where(counts > 0, (ends - 1) // MOE_TILE - first_tile + 1, 0)
    vend = jnp.cumsum(nvis)
    vstart = vend - nvis
    v = jnp.arange(nb, dtype=jnp.int32)
    ve = jnp.minimum(jnp.sum(v[:, None] >= vend[None, :], axis=1, dtype=jnp.int32), N_EXPERTS - 1)
    onehot = (ve[:, None] == eids[None, :]).astype(jnp.int32)
    pick = lambda tab: jnp.sum(onehot * tab[None, :], axis=1, dtype=jnp.int32)
    live = v < vend[N_EXPERTS - 1]
    vt = jnp.where(live, pick(first_tile) + v - pick(vstart), n_tiles - 1)
    vlo = jnp.where(live, pick(starts), 0)
    vhi = jnp.where(live, pick(ends), 0)
    return (vt, ve, vlo, vhi, tok_s.reshape(n_tiles, 1, MOE_TILE), dst_s.reshape(n_tiles, 1, MOE_TILE),
            g_s.reshape(A, 1))


def _rope_tables(S, d, tm):
    n = tm // d
    j = jnp.arange(S, dtype=jnp.int32)
    tile, rem = j // tm, j % tm
    pos = ((tile * n + rem % n) * d + rem // n).astype(F32)
    inv_freq = 1.0 / (ROPE_THETA ** (jnp.arange(0, HEAD_DIM, 2, dtype=F32) / HEAD_DIM))
    ang = pos[:, None] * inv_freq[None, :]
    cos, sin = jnp.cos(ang), jnp.sin(ang)
    return jnp.tile(cos, (1, 4)), jnp.concatenate([-sin, sin, -sin, sin], axis=1)


def _trunk(x, p, *, B, S):
    T = B * S
    x = x.reshape(T, D_MODEL)
    tm_qkv = 512
    tables = [_rope_tables(S, d, tm_qkv) for d in DILATIONS]
    y2 = None
    for i in range(DEPTH):
        gm = p["mixer_norm"][i][None, :]
        fg = p["ffn_norm"][i][None, :]
        rw, rb = p["router_w"][i], p["router_b"][i]
        if i % 2 == 0:
            li = i // 2
            os_, ls_ = [], []
            adds = [] if y2 is None else [y2, y2]
            xs = x
            for gi_, d in enumerate(DILATIONS):
                emit = bool(adds) and gi_ == 0
                res = _qkv_call(x, adds, gm, p["w_qkv"][li][gi_], tables[gi_][0], tables[gi_][1],
                                B=B, S=S, d=d, emit_x=emit, tm=tm_qkv)
                if emit:
                    xs, qkv = res
                else:
                    (qkv,) = res
                L = S // d
                o, lse = _attn_call(qkv.reshape(B * d, L, 3 * D_MODEL), L=L)
                os_.append(o.reshape(B, d, L, D_MODEL))
                ls_.append(lse.reshape(B, d, L, 128))
            x, h2, route = _oproj_call(os_, ls_, p["w_o"][li], xs, fg, rw, rb, B=B, S=S)
        else:
            li = i // 2
            x, h2, route = _pool_call(x, y2, gm, p["w_pool"][li], p["pool_scale"][li][None, :], fg, rw, rb,
                                      B=B, S=S)
        y2 = _moe_call(h2, _plan(route, T), p["w_gate"][i], p["w_up"][i], p["w_down"][i], T=T)
    out = _final_call(x, y2, p["final_norm"][None, :], T=T)
    return out.reshape(B, S, D_MODEL)


def kernel(x_prompt, x_sample, mixer_norm, w_qkv, w_o, w_pool, pool_scale, ffn_norm, router_group_w,
           router_group_b, router_expert_w, router_expert_b, w_gate, w_up, w_down, final_norm):
    n_attn = w_qkv.shape[0]
    wq = w_qkv.astype(BF16).reshape(n_attn, D_MODEL, len(DILATIONS), 3 * D_MODEL).transpose(0, 2, 1, 3)
    pad = ROUTER_LANES - N_GROUPS - N_EXPERTS
    router_w = jnp.pad(jnp.concatenate([router_group_w, router_expert_w], axis=-1),
                       ((0, 0), (0, 0), (0, pad))).astype(BF16)
    router_b = jnp.pad(jnp.concatenate([router_group_b, router_expert_b], axis=-1), ((0, 0), (0, pad)))[:, None, :]
    p = dict(mixer_norm=mixer_norm, ffn_norm=ffn_norm, final_norm=final_norm, pool_scale=pool_scale,
             w_qkv=wq, w_o=w_o.astype(BF16), w_pool=w_pool.astype(BF16),
             router_w=router_w, router_b=router_b,
             w_gate=w_gate.astype(BF16), w_up=w_up.astype(BF16), w_down=w_down.astype(BF16))
    y_prompt = _trunk(x_prompt, p, B=x_prompt.shape[0], S=x_prompt.shape[1])
    y_sample = _trunk(x_sample, p, B=x_sample.shape[0], S=x_sample.shape[1])
    return (y_prompt, y_sample)
```

```python
import functools

import jax
import jax.numpy as jnp
from jax import lax
from jax.experimental import pallas as pl
from jax.experimental.pallas import tpu as pltpu

F32 = jnp.float32
BF16 = jnp.bfloat16

D_MODEL = 1024
DEPTH = 4
DILATIONS = (1, 4, 16)
HALF = 64
N_HEADS = 16
HEAD_DIM = 64
Q_BLOCK = 128
ROPE_THETA = 10000.0
NEG_INF = -1e30
POOL_WINDOWS = (2, 4, 8, 16)
POOL_CH = 256
N_GROUPS = 4
EPG = 8
N_EXPERTS = 32
TOP_K = 2
D_EXPERT = 512
MOE_TILE = 256
RMS_EPS = 1e-6
ROUTER_LANES = 128
HALO = 8
VMEM_LIMIT = 48 * 1024 * 1024


def _rms(x, g):
    return x * lax.rsqrt(jnp.mean(x * x, axis=-1, keepdims=True) + RMS_EPS) * g


def _cparams(sem):
    return pltpu.CompilerParams(dimension_semantics=sem, vmem_limit_bytes=VMEM_LIMIT)


LANE_CHUNKS = D_MODEL // 128


def _read_tile_rows(ref, n, start=0):
    return jnp.concatenate([ref[pl.ds(start * LANE_CHUNKS + j, n, stride=LANE_CHUNKS), :]
                            for j in range(LANE_CHUNKS)], axis=1)


def _write_tile_rows(ref, val, start=0):
    n = val.shape[0]
    for j in range(LANE_CHUNKS):
        ref[pl.ds(start * LANE_CHUNKS + j, n, stride=LANE_CHUNKS), :] = val[:, 128 * j:128 * (j + 1)]


def _qkv_kernel(*refs, n_add, d, tm, emit_x):
    x_ref = refs[0]
    add_refs = refs[1:1 + n_add]
    g_ref, w_ref, cos_ref, sin_ref = refs[1 + n_add:5 + n_add]
    outs = refs[5 + n_add:]
    if emit_x:
        xs_ref, out_ref, hperm, hf = outs
    else:
        out_ref, hperm, hf = outs
    n = tm // d

    xs = x_ref[...]
    for a in add_refs:
        xs = xs + _read_tile_rows(a, tm)
    if emit_x:
        xs_ref[...] = xs
    h = _rms(xs, g_ref[...])
    if d == 1:
        hperm[...] = h.astype(BF16)
    else:
        for j in range(D_MODEL // 128):
            hf[j] = h[:, 128 * j:128 * (j + 1)]
        for r in range(d):
            for j in range(D_MODEL // 128):
                hperm[r * n:(r + 1) * n, 128 * j:128 * (j + 1)] = hf[j, pl.ds(r, n, stride=d), :].astype(BF16)

    CW = 256

    def project(col):
        return jnp.dot(hperm[...], w_ref[:, col:col + CW], preferred_element_type=F32)

    def store(col, res):
        for r in range(d):
            out_ref[r, :, col:col + res.shape[1]] = res[r * n:(r + 1) * n, :]

    lane = lax.broadcasted_iota(jnp.int32, (tm, 128), 1)
    low = (lane % HEAD_DIM) < (HEAD_DIM // 2)

    def rope(base, scale):
        for j in range(D_MODEL // CW):
            acc = project(base + CW * j)
            for k in range(CW // 128):
                xj = acc[:, 128 * k:128 * (k + 1)]
                partner = jnp.where(low, pltpu.roll(xj, 128 - HEAD_DIM // 2, 1), pltpu.roll(xj, HEAD_DIM // 2, 1))
                o = xj * cos_ref[...] + partner * sin_ref[...]
                if scale != 1.0:
                    o = o * scale
                store(base + CW * j + 128 * k, o.astype(BF16))

    rope(0, HEAD_DIM ** -0.5)
    rope(D_MODEL, 1.0)
    for j in range(D_MODEL // CW):
        store(2 * D_MODEL + CW * j, project(2 * D_MODEL + CW * j).astype(BF16))


def _qkv_call(x, adds, g, w, cos, sin, *, B, S, d, emit_x, tm=512):
    T = B * S
    nI = S // tm
    L = S // d
    n = tm // d
    n_add = len(adds)
    row = pl.BlockSpec((tm, D_MODEL), lambda i: (i, 0))
    in_specs = [row]
    T_rows = T // tm
    for k in range(n_add):
        in_specs.append(pl.BlockSpec((tm * LANE_CHUNKS, 128), lambda i, k=k: (k * T_rows + i, 0)))
    in_specs += [
        pl.BlockSpec((1, D_MODEL), lambda i: (0, 0)),
        pl.BlockSpec((D_MODEL, 3 * D_MODEL), lambda i: (0, 0)),
        pl.BlockSpec((tm, 128), lambda i: (i % nI, 0)),
        pl.BlockSpec((tm, 128), lambda i: (i % nI, 0)),
    ]
    out_spec = pl.BlockSpec((None, d, n, 3 * D_MODEL), lambda i: (i // nI, 0, i % nI, 0))
    out_shape = jax.ShapeDtypeStruct((B, d, L, 3 * D_MODEL), BF16)
    if emit_x:
        out_specs = [row, out_spec]
        out_shapes = [jax.ShapeDtypeStruct((T, D_MODEL), F32), out_shape]
    else:
        out_specs = [out_spec]
        out_shapes = [out_shape]
    res = pl.pallas_call(
        functools.partial(_qkv_kernel, n_add=n_add, d=d, tm=tm, emit_x=emit_x),
        grid=(T // tm,),
        in_specs=in_specs,
        out_specs=out_specs,
        out_shape=out_shapes,
        scratch_shapes=[pltpu.VMEM((tm, D_MODEL), BF16), pltpu.VMEM((D_MODEL // 128, tm, 128), F32)],
        compiler_params=_cparams(("parallel",)),
        name=f"qkv_d{d}",
    )(x, *adds, g, w, cos, sin)
    return res


def _attn_kernel(q_ref, kp_ref, kc_ref, kn_ref, vp_ref, vc_ref, vn_ref, o_ref, lse_ref, kbuf, vbuf, *, L):
    i = pl.program_id(1)
    W = Q_BLOCK + 2 * HALF
    kbuf[0:HALF, :] = kp_ref[...]
    kbuf[HALF:HALF + Q_BLOCK, :] = kc_ref[...]
    kbuf[HALF + Q_BLOCK:W, :] = kn_ref[...]
    vbuf[0:HALF, :] = vp_ref[...]
    vbuf[HALF:HALF + Q_BLOCK, :] = vc_ref[...]
    vbuf[HALF + Q_BLOCK:W, :] = vn_ref[...]
    QQ = 2 * Q_BLOCK
    row = lax.broadcasted_iota(jnp.int32, (QQ, W), 0) % Q_BLOCK
    col = lax.broadcasted_iota(jnp.int32, (QQ, W), 1)
    rel = col - row
    kpos = i * Q_BLOCK - HALF + col
    mask = (rel >= 0) & (rel <= 2 * HALF) & (kpos >= 0) & (kpos < L)
    lane = lax.broadcasted_iota(jnp.int32, (Q_BLOCK, 128), 1)
    first = lane < HEAD_DIM
    keep_a = first.astype(F32).astype(BF16)
    keep_b = 1 - keep_a
    lse_tile = jnp.zeros((Q_BLOCK, 128), F32)
    for j in range(N_HEADS // 2):
        sl = slice(128 * j, 128 * (j + 1))
        q2 = q_ref[:, sl]
        qq = jnp.concatenate([q2 * keep_a, q2 * keep_b], axis=0)
        s = lax.dot_general(qq, kbuf[:, sl], (((1,), (1,)), ((), ())), preferred_element_type=F32)
        s = jnp.where(mask, s, NEG_INF)
        m = jnp.max(s, axis=-1, keepdims=True)
        p = jnp.exp(s - m)
        l = jnp.sum(p, axis=-1, keepdims=True)
        o = jnp.dot(p.astype(BF16), vbuf[:, sl], preferred_element_type=F32) / l
        o_ref[:, sl] = jnp.where(first, o[:Q_BLOCK], o[Q_BLOCK:]).astype(BF16)
        lse = m + jnp.log(l)
        lse_tile = jnp.where(lane == 2 * j, lse[:Q_BLOCK],
                             jnp.where(lane == 2 * j + 1, lse[Q_BLOCK:], lse_tile))
    lse_ref[...] = lse_tile


def _attn_call(qkv, *, L):
    NS = qkv.shape[0]
    nq = L // Q_BLOCK
    nh = L // HALF
    q_spec = pl.BlockSpec((None, Q_BLOCK, D_MODEL), lambda s, i: (s, i, 0))

    def cur(cb):
        return pl.BlockSpec((None, Q_BLOCK, D_MODEL), lambda s, i: (s, i, cb))

    def prev(cb):
        return pl.BlockSpec((None, HALF, D_MODEL), lambda s, i: (s, jnp.maximum(2 * i - 1, 0), cb))

    def nxt(cb):
        return pl.BlockSpec((None, HALF, D_MODEL), lambda s, i: (s, jnp.minimum(2 * i + 2, nh - 1), cb))

    return pl.pallas_call(
        functools.partial(_attn_kernel, L=L),
        grid=(NS, nq),
        in_specs=[q_spec, prev(1), cur(1), nxt(1), prev(2), cur(2), nxt(2)],
        out_specs=[pl.BlockSpec((None, Q_BLOCK, D_MODEL), lambda s, i: (s, i, 0)),
                   pl.BlockSpec((None, Q_BLOCK, 128), lambda s, i: (s, i, 0))],
        out_shape=[jax.ShapeDtypeStruct((NS, L, D_MODEL), BF16),
                   jax.ShapeDtypeStruct((NS, L, 128), F32)],
        scratch_shapes=[pltpu.VMEM((Q_BLOCK + 2 * HALF, D_MODEL), BF16),
                        pltpu.VMEM((Q_BLOCK + 2 * HALF, D_MODEL), BF16)],
        compiler_params=_cparams(("parallel", "parallel")),
        name="band_attn",
    )(qkv, qkv, qkv, qkv, qkv, qkv, qkv)


def _post(x_new, fg_ref, rw_ref, rb_ref, xn_ref, h2_ref, rt_ref):
    xn_ref[...] = x_new
    h2 = _rms(x_new, fg_ref[...])
    _write_tile_rows(h2_ref, h2)
    lg = jnp.dot(h2.astype(BF16), rw_ref[...], preferred_element_type=F32) + rb_ref[...]
    lane = lax.broadcasted_iota(jnp.int32, lg.shape, 1)

    def first_argmax(v, vmax):
        return jnp.min(jnp.where(v == vmax, lane, ROUTER_LANES), axis=-1, keepdims=True)

    is_g = lane < N_GROUPS
    vg = jnp.where(is_g, lg, -jnp.inf)
    gmax = jnp.max(vg, axis=-1, keepdims=True)
    gi = first_argmax(vg, gmax)
    gp = 1.0 / jnp.sum(jnp.where(is_g, jnp.exp(lg - gmax), 0.0), axis=-1, keepdims=True)
    lo = N_GROUPS + EPG * gi
    in_grp = (lane >= lo) & (lane < lo + EPG)
    ve = jnp.where(in_grp, lg, -jnp.inf)
    m1 = jnp.max(ve, axis=-1, keepdims=True)
    i1 = first_argmax(ve, m1)
    ve2 = jnp.where(lane == i1, -jnp.inf, ve)
    m2 = jnp.max(ve2, axis=-1, keepdims=True)
    i2 = first_argmax(ve2, m2)
    esum = jnp.sum(jnp.where(in_grp, jnp.exp(lg - m1), 0.0), axis=-1, keepdims=True)
    ep1 = 1.0 / esum
    ep2 = jnp.exp(m2 - m1) / esum
    den = ep1 + ep2
    g1 = gp * ep1 / den
    g2 = gp * ep2 / den
    e1 = (i1 - N_GROUPS).astype(F32)
    e2 = (i2 - N_GROUPS).astype(F32)
    rt_ref[...] = jnp.where(lane == 0, e1, jnp.where(lane == 1, e2, jnp.where(lane == 2, g1,
                            jnp.where(lane == 3, g2, 0.0))))


def _oproj_kernel(o0, o1, o2, l0, l1, l2, wo_ref, x_ref, fg_ref, rw_ref, rb_ref,
                  xn_ref, h2_ref, lg_ref, onat, lnat, mbuf, *, tm):
    o_refs = (o0, o1, o2)
    l_refs = (l0, l1, l2)
    for g, d in enumerate(DILATIONS):
        n = tm // d
        for r in range(d):
            if d == 1:
                lnat[g] = l_refs[g][r]
                for j in range(D_MODEL // 128):
                    onat[g, j] = o_refs[g][r, :, 128 * j:128 * (j + 1)].astype(F32)
            else:
                lnat[g, pl.ds(r, n, stride=d), :] = l_refs[g][r]
                for j in range(D_MODEL // 128):
                    onat[g, j, pl.ds(r, n, stride=d), :] = o_refs[g][r, :, 128 * j:128 * (j + 1)].astype(F32)
    ls = [lnat[g] for g in range(3)]
    m = jnp.maximum(jnp.maximum(ls[0], ls[1]), ls[2])
    es = [jnp.exp(l - m) for l in ls]
    tot = es[0] + es[1] + es[2]
    al = [e / tot for e in es]
    lane = lax.broadcasted_iota(jnp.int32, (tm, 128), 1)
    first = lane < HEAD_DIM
    for j in range(D_MODEL // 128):
        acc = None
        for g in range(3):
            a = jnp.where(first, al[g][:, 2 * j:2 * j + 1], al[g][:, 2 * j + 1:2 * j + 2])
            t = a * onat[g, j]
            acc = t if acc is None else acc + t
        mbuf[:, 128 * j:128 * (j + 1)] = acc.astype(BF16)
    y = jnp.dot(mbuf[...], wo_ref[...], preferred_element_type=F32)
    _post(x_ref[...] + y, fg_ref, rw_ref, rb_ref, xn_ref, h2_ref, lg_ref)


def _oproj_call(os_, ls_, wo, x, fg, rw, rb, *, B, S, tm=256):
    T = B * S
    nI = S // tm
    in_specs = []
    for d in DILATIONS:
        in_specs.append(pl.BlockSpec((None, d, tm // d, D_MODEL), lambda i: (i // nI, 0, i % nI, 0)))
    for d in DILATIONS:
        in_specs.append(pl.BlockSpec((None, d, tm // d, 128), lambda i: (i // nI, 0, i % nI, 0)))
    row = pl.BlockSpec((tm, D_MODEL), lambda i: (i, 0))
    in_specs += [
        pl.BlockSpec((D_MODEL, D_MODEL), lambda i: (0, 0)),
        row,
        pl.BlockSpec((1, D_MODEL), lambda i: (0, 0)),
        pl.BlockSpec((D_MODEL, ROUTER_LANES), lambda i: (0, 0)),
        pl.BlockSpec((1, ROUTER_LANES), lambda i: (0, 0)),
    ]
    return pl.pallas_call(
        functools.partial(_oproj_kernel, tm=tm),
        grid=(T // tm,),
        in_specs=in_specs,
        out_specs=[row, pl.BlockSpec((tm * LANE_CHUNKS, 128), lambda i: (i, 0)),
                   pl.BlockSpec((tm, ROUTER_LANES), lambda i: (i, 0))],
        out_shape=[jax.ShapeDtypeStruct((T, D_MODEL), F32), jax.ShapeDtypeStruct((T * LANE_CHUNKS, 128), F32),
                   jax.ShapeDtypeStruct((T, ROUTER_LANES), F32)],
        scratch_shapes=[pltpu.VMEM((3, D_MODEL // 128, tm, 128), F32), pltpu.VMEM((3, tm, 128), F32),
                        pltpu.VMEM((tm, D_MODEL), BF16)],
        compiler_params=_cparams(("parallel",)),
        name="merge_oproj",
    )(*os_, *ls_, wo, x, fg, rw, rb)


def _pool_kernel(x_ref, ya_ref, yb_ref, xp_ref, yap_ref, ybp_ref, xq_ref, yaq_ref, ybq_ref,
                 g_ref, wp_ref, ps_ref, fg_ref, rw_ref, rb_ref, xn_ref, h2_ref, lg_ref, hp, xnew, *, tm, S):
    i = pl.program_id(0)
    nI = S // tm
    ii = i % nI
    g = g_ref[...]
    xs = x_ref[...] + _read_tile_rows(ya_ref, tm) + _read_tile_rows(yb_ref, tm)
    h = _rms(xs, g)
    hprev = _rms(xp_ref[...] + _read_tile_rows(yap_ref, HALO) + _read_tile_rows(ybp_ref, HALO), g)
    hnext = _rms(xq_ref[...] + _read_tile_rows(yaq_ref, HALO) + _read_tile_rows(ybq_ref, HALO), g)
    hp[0:HALO, :] = jnp.where(ii > 0, hprev, 0.0)
    hp[HALO:HALO + tm, :] = h
    hp[HALO + tm:2 * HALO + tm, :] = jnp.where(ii < nI - 1, hnext, 0.0)
    t = ii * tm + lax.broadcasted_iota(jnp.int32, (tm, 1), 0)
    for k, w in enumerate(POOL_WINDOWS):
        cols = slice(k * POOL_CH, (k + 1) * POOL_CH)
        acc = None
        for j in range(-(w // 2), w // 2):
            v = hp[HALO + j:HALO + j + tm, cols]
            acc = v if acc is None else acc + v
        lo = jnp.clip(t - w // 2, 0, S)
        hi = jnp.clip(t - w // 2 + w, 0, S)
        mean = acc / (hi - lo).astype(F32)
        diff = (mean - h[:, cols]).astype(BF16)
        out = jnp.dot(diff, wp_ref[k], preferred_element_type=F32)
        xnew[:, cols] = xs[:, cols] + out * ps_ref[:, cols]
    _post(xnew[...], fg_ref, rw_ref, rb_ref, xn_ref, h2_ref, lg_ref)


def _pool_call(x, y2, g, wp, ps, fg, rw, rb, *, B, S, tm=512):
    T = B * S
    nb8 = T // HALO
    r8 = tm // HALO
    nT = T // tm

    def main(k, shape=(tm, D_MODEL)):
        return pl.BlockSpec(shape, lambda i: (k * nT + i, 0))

    def prev(k, shape=(HALO, D_MODEL)):
        return pl.BlockSpec(shape, lambda i: (k * nb8 + jnp.maximum(i * r8 - 1, 0), 0))

    def nxt(k, shape=(HALO, D_MODEL)):
        return pl.BlockSpec(shape, lambda i: (k * nb8 + jnp.minimum((i + 1) * r8, nb8 - 1), 0))

    row = pl.BlockSpec((tm, D_MODEL), lambda i: (i, 0))
    vec = pl.BlockSpec((1, D_MODEL), lambda i: (0, 0))
    tr_main = (tm * LANE_CHUNKS, 128)
    tr_halo = (HALO * LANE_CHUNKS, 128)
    in_specs = [main(0), main(0, tr_main), main(1, tr_main), prev(0), prev(0, tr_halo), prev(1, tr_halo),
                nxt(0), nxt(0, tr_halo), nxt(1, tr_halo),
                vec, pl.BlockSpec((len(POOL_WINDOWS), POOL_CH, POOL_CH), lambda i: (0, 0, 0)), vec, vec,
                pl.BlockSpec((D_MODEL, ROUTER_LANES), lambda i: (0, 0)),
                pl.BlockSpec((1, ROUTER_LANES), lambda i: (0, 0))]
    return pl.pallas_call(
        functools.partial(_pool_kernel, tm=tm, S=S),
        grid=(nT,),
        in_specs=in_specs,
        out_specs=[row, pl.BlockSpec(tr_main, lambda i: (i, 0)),
                   pl.BlockSpec((tm, ROUTER_LANES), lambda i: (i, 0))],
        out_shape=[jax.ShapeDtypeStruct((T, D_MODEL), F32), jax.ShapeDtypeStruct((T * LANE_CHUNKS, 128), F32),
                   jax.ShapeDtypeStruct((T, ROUTER_LANES), F32)],
        scratch_shapes=[pltpu.VMEM((tm + 2 * HALO, D_MODEL), F32), pltpu.VMEM((tm, D_MODEL), F32)],
        compiler_params=_cparams(("parallel",)),
        name="pool_mixer",
    )(x, y2, y2, x, y2, y2, x, y2, y2, g, wp, ps, fg, rw, rb)


def _moe_kernel(vt_ref, ve_ref, vlo_ref, vhi_ref, tokc_ref, tokn_ref, dst_ref, gate_ref, h_hbm,
                wg_ref, wu_ref, wd_ref, y_hbm, xbuf, ybuf, gsem, ssem, *, T):
    del ve_ref
    i = pl.program_id(0)
    nb = pl.num_programs(0)
    slot = i % 2
    base = vt_ref[i] * MOE_TILE
    lo = vlo_ref[i]
    hi = vhi_ref[i]
    junk = 2 * T + slot * MOE_TILE
    RC = LANE_CHUNKS

    def row_in(tok, s, r):
        return pltpu.make_async_copy(h_hbm.at[pl.ds(pl.multiple_of(tok * RC, RC), RC), :],
                                     xbuf.at[s, pl.ds(r * RC, RC), :], gsem.at[s])

    def row_out(row, s, r):
        return pltpu.make_async_copy(ybuf.at[s, pl.ds(r * RC, RC), :],
                                     y_hbm.at[pl.ds(pl.multiple_of(row * RC, RC), RC), :], ssem.at[s])

    def gather(tok_ref, s):
        for r in range(MOE_TILE):
            row_in(tok_ref[0, r], s, r).start(priority=r % 2)

    def gather_wait(s):
        for r in range(MOE_TILE):
            row_in(0, s, r).wait()

    def scatter_wait(s):
        for r in range(MOE_TILE):
            row_out(0, s, r).wait()

    @pl.when(i == 0)
    def _():
        gather(tokc_ref, 0)

    @pl.when(i + 1 < nb)
    def _():
        gather(tokn_ref, 1 - slot)

    gather_wait(slot)

    @pl.when(i >= 2)
    def _():
        scatter_wait(slot)

    x = _read_tile_rows(xbuf.at[slot], MOE_TILE).astype(BF16)
    g = jnp.dot(x, wg_ref[...], preferred_element_type=F32)
    u = jnp.dot(x, wu_ref[...], preferred_element_type=F32)
    a = (g * (1.0 / (1.0 + jnp.exp(-g)))) * u
    y = jnp.dot(a.astype(BF16), wd_ref[...], preferred_element_type=F32)
    _write_tile_rows(ybuf.at[slot], y * gate_ref[...])

    whole = (base >= lo) & (base + MOE_TILE <= hi)

    @pl.when(whole)
    def _():
        for r in range(MOE_TILE):
            row_out(dst_ref[0, r], slot, r).start(priority=r % 2)

    @pl.when(jnp.logical_not(whole))
    def _():
        for r in range(MOE_TILE):
            mine = (base + r >= lo) & (base + r < hi)
            row_out(jnp.where(mine, dst_ref[0, r], junk + r), slot, r).start(priority=r % 2)

    @pl.when(i == nb - 1)
    def _():
        scatter_wait(slot)

        @pl.when(nb >= 2)
        def _():
            scatter_wait(1 - slot)


def _moe_call(h2, plan, wg, wu, wd, *, T, layer):
    vt, ve, vlo, vhi, tok, dst, gate = plan
    nb = vt.shape[0]
    smem_row = functools.partial(pl.BlockSpec, (None, 1, MOE_TILE), memory_space=pltpu.SMEM)
    buf = pltpu.VMEM((2, MOE_TILE * LANE_CHUNKS, 128), F32)
    grid_spec = pltpu.PrefetchScalarGridSpec(
        num_scalar_prefetch=4,
        grid=(nb,),
        in_specs=[
            smem_row(lambda i, vt, ve, lo, hi: (vt[i], 0, 0)),
            smem_row(lambda i, vt, ve, lo, hi: (vt[jnp.minimum(i + 1, nb - 1)], 0, 0)),
            smem_row(lambda i, vt, ve, lo, hi: (vt[i], 0, 0)),
            pl.BlockSpec((MOE_TILE, 1), lambda i, vt, ve, lo, hi: (vt[i], 0)),
            pl.BlockSpec(memory_space=pl.ANY),
            pl.BlockSpec((None, None, D_MODEL, D_EXPERT), lambda i, vt, ve, lo, hi: (layer, ve[i], 0, 0)),
            pl.BlockSpec((None, None, D_MODEL, D_EXPERT), lambda i, vt, ve, lo, hi: (layer, ve[i], 0, 0)),
            pl.BlockSpec((None, None, D_EXPERT, D_MODEL), lambda i, vt, ve, lo, hi: (layer, ve[i], 0, 0)),
        ],
        out_specs=pl.BlockSpec(memory_space=pl.ANY),
        scratch_shapes=[buf, buf, pltpu.SemaphoreType.DMA((2,)), pltpu.SemaphoreType.DMA((2,))],
    )
    return pl.pallas_call(
        functools.partial(_moe_kernel, T=T),
        grid_spec=grid_spec,
        out_shape=jax.ShapeDtypeStruct(((2 * T + 2 * MOE_TILE) * LANE_CHUNKS, 128), F32),
        compiler_params=_cparams(("arbitrary",)),
        name="moe_experts",
    )(vt, ve, vlo, vhi, tok, tok, dst, gate, h2, wg, wu, wd)


def _final_kernel(x_ref, ya_ref, yb_ref, g_ref, o_ref, *, tm):
    o_ref[...] = _rms(x_ref[...] + _read_tile_rows(ya_ref, tm) + _read_tile_rows(yb_ref, tm), g_ref[...])


def _final_call(x, y2, g, *, T, tm=512):
    nT = T // tm
    return pl.pallas_call(
        functools.partial(_final_kernel, tm=tm),
        grid=(nT,),
        in_specs=[pl.BlockSpec((tm, D_MODEL), lambda i: (i, 0)),
                  pl.BlockSpec((tm * LANE_CHUNKS, 128), lambda i: (i, 0)),
                  pl.BlockSpec((tm * LANE_CHUNKS, 128), lambda i: (nT + i, 0)),
                  pl.BlockSpec((1, D_MODEL), lambda i: (0, 0))],
        out_specs=pl.BlockSpec((tm, D_MODEL), lambda i: (i, 0)),
        out_shape=jax.ShapeDtypeStruct((T, D_MODEL), F32),
        compiler_params=_cparams(("parallel",)),
        name="final_norm",
    )(x, y2, y2, g)


def _plan(route, T):
    A = T * TOP_K
    n_tiles = A // MOE_TILE
    nb = n_tiles + N_EXPERTS - 1
    e_flat = route[:, 0:TOP_K].astype(jnp.int32).reshape(A)
    g_flat = route[:, TOP_K:2 * TOP_K].reshape(A)
    a_iota = jnp.arange(A, dtype=jnp.int32)
    _, a_s, g_s = lax.sort((e_flat, a_iota, g_flat), num_keys=1, is_stable=True)
    tok_s = a_s // TOP_K
    dst_s = (a_s % TOP_K) * T + tok_s
    eids = jnp.arange(N_EXPERTS, dtype=jnp.int32)
    counts = jnp.sum(e_flat[:, None] == eids[None, :], axis=0, dtype=jnp.int32)
    ends = jnp.cumsum(counts)
    starts = ends - counts
    first_tile = starts // MOE_TILE
    nvis = jnp.where(counts > 0, (ends - 1) // MOE_TILE - first_tile + 1, 0)
    vend = jnp.cumsum(nvis)
    vstart = vend - nvis
    v = jnp.arange(nb, dtype=jnp.int32)
    ve = jnp.minimum(jnp.sum(v[:, None] >= vend[None, :], axis=1, dtype=jnp.int32), N_EXPERTS - 1)
    onehot = (ve[:, None] == eids[None, :]).astype(jnp.int32)
    pick = lambda tab: jnp.sum(onehot * tab[None, :], axis=1, dtype=jnp.int32)
    live = v < vend[N_EXPERTS - 1]
    vt = jnp.where(live, pick(first_tile) + v - pick(vstart), n_tiles - 1)
    vlo = jnp.where(live, pick(starts), 0)
    vhi = jnp.where(live, pick(ends), 0)
    return (vt, ve, vlo, vhi, tok_s.reshape(n_tiles, 1, MOE_TILE), dst_s.reshape(n_tiles, 1, MOE_TILE),
            g_s.reshape(A, 1))


def _rope_tables(S, d, tm):
    n = tm // d
    j = jnp.arange(S, dtype=jnp.int32)
    tile, rem = j // tm, j % tm
    pos = ((tile * n + rem % n) * d + rem // n).astype(F32)
    inv_freq = 1.0 / (ROPE_THETA ** (jnp.arange(0, HEAD_DIM, 2, dtype=F32) / HEAD_DIM))
    ang = pos[:, None] * inv_freq[None, :]
    cos, sin = jnp.cos(ang), jnp.sin(ang)
    return jnp.tile(cos, (1, 4)), jnp.concatenate([-sin, sin, -sin, sin], axis=1)


def _trunk(x, p, *, B, S):
    T = B * S
    x = x.reshape(T, D_MODEL)
    tm_qkv = 512
    tables = [_rope_tables(S, d, tm_qkv) for d in DILATIONS]
    y2 = None
    for i in range(DEPTH):
        gm = p["mixer_norm"][i][None, :]
        fg = p["ffn_norm"][i][None, :]
        rw, rb = p["router_w"][i], p["router_b"][i]
        if i % 2 == 0:
            li = i // 2
            os_, ls_ = [], []
            adds = [] if y2 is None else [y2, y2]
            xs = x
            for gi_, d in enumerate(DILATIONS):
                emit = bool(adds) and gi_ == 0
                res = _qkv_call(x, adds, gm, p["w_qkv"][li][gi_], tables[gi_][0], tables[gi_][1],
                                B=B, S=S, d=d, emit_x=emit, tm=tm_qkv)
                if emit:
                    xs, qkv = res
                else:
                    (qkv,) = res
                L = S // d
                o, lse = _attn_call(qkv.reshape(B * d, L, 3 * D_MODEL), L=L)
                os_.append(o.reshape(B, d, L, D_MODEL))
                ls_.append(lse.reshape(B, d, L, 128))
            x, h2, route = _oproj_call(os_, ls_, p["w_o"][li], xs, fg, rw, rb, B=B, S=S)
        else:
            li = i // 2
            x, h2, route = _pool_call(x, y2, gm, p["w_pool"][li], p["pool_scale"][li][None, :], fg, rw, rb,
                                      B=B, S=S)
        y2 = _moe_call(h2, _plan(route, T), p["w_gate"], p["w_up"], p["w_down"], T=T, layer=i)
    out = _final_call(x, y2, p["final_norm"][None, :], T=T)
    return out.reshape(B, S, D_MODEL)


def kernel(x_prompt, x_sample, mixer_norm, w_qkv, w_o, w_pool, pool_scale, ffn_norm, router_group_w,
           router_group_b, router_expert_w, router_expert_b, w_gate, w_up, w_down, final_norm):
    n_attn = w_qkv.shape[0]
    wq = w_qkv.astype(BF16).reshape(n_attn, D_MODEL, len(DILATIONS), 3 * D_MODEL).transpose(0, 2, 1, 3)
    pad = ROUTER_LANES - N_GROUPS - N_EXPERTS
    router_w = jnp.pad(jnp.concatenate([router_group_w, router_expert_w], axis=-1),
                       ((0, 0), (0, 0), (0, pad))).astype(BF16)
    router_b = jnp.pad(jnp.concatenate([router_group_b, router_expert_b], axis=-1), ((0, 0), (0, pad)))[:, None, :]
    p = dict(mixer_norm=mixer_norm, ffn_norm=ffn_norm, final_norm=final_norm, pool_scale=pool_scale,
             w_qkv=wq, w_o=w_o.astype(BF16), w_pool=w_pool.astype(BF16),
             router_w=router_w, router_b=router_b,
             w_gate=w_gate.astype(BF16), w_up=w_up.astype(BF16), w_down=w_down.astype(BF16))
    y_prompt = _trunk(x_prompt, p, B=x_prompt.shape[0], S=x_prompt.shape[1])
    y_sample = _trunk(x_sample, p, B=x_sample.shape[0], S=x_sample.shape[1])
    return (y_prompt, y_sample)
```

```python
import functools

import jax
import jax.numpy as jnp
from jax import lax
from jax.experimental import pallas as pl
from jax.experimental.pallas import tpu as pltpu

F32 = jnp.float32
BF16 = jnp.bfloat16

D_MODEL = 1024
DEPTH = 4
DILATIONS = (1, 4, 16)
HALF = 64
N_HEADS = 16
HEAD_DIM = 64
Q_BLOCK = 128
ROPE_THETA = 10000.0
NEG_INF = -1e30
POOL_WINDOWS = (2, 4, 8, 16)
POOL_CH = 256
N_GROUPS = 4
EPG = 8
N_EXPERTS = 32
TOP_K = 2
D_EXPERT = 512
MOE_TILE = 128
RMS_EPS = 1e-6
ROUTER_LANES = 128
HALO = 8
VMEM_LIMIT = 48 * 1024 * 1024


def _rms(x, g):
    return x * lax.rsqrt(jnp.mean(x * x, axis=-1, keepdims=True) + RMS_EPS) * g


def _cparams(sem):
    return pltpu.CompilerParams(dimension_semantics=sem, vmem_limit_bytes=VMEM_LIMIT)


LANE_CHUNKS = D_MODEL // 128


def _read_tile_rows(ref, n, start=0):
    return jnp.concatenate([ref[pl.ds(start * LANE_CHUNKS + j, n, stride=LANE_CHUNKS), :]
                            for j in range(LANE_CHUNKS)], axis=1)


def _write_tile_rows(ref, val, start=0):
    n = val.shape[0]
    for j in range(LANE_CHUNKS):
        ref[pl.ds(start * LANE_CHUNKS + j, n, stride=LANE_CHUNKS), :] = val[:, 128 * j:128 * (j + 1)]


def _qkv_kernel(*refs, n_add, d, tm, emit_x):
    x_ref = refs[0]
    add_refs = refs[1:1 + n_add]
    g_ref, w_ref, cos_ref, sin_ref = refs[1 + n_add:5 + n_add]
    outs = refs[5 + n_add:]
    if emit_x:
        xs_ref, out_ref, hperm, hf = outs
    else:
        out_ref, hperm, hf = outs
    n = tm // d

    xs = x_ref[...]
    for a in add_refs:
        xs = xs + _read_tile_rows(a, tm)
    if emit_x:
        xs_ref[...] = xs
    h = _rms(xs, g_ref[...])
    if d == 1:
        hperm[...] = h.astype(BF16)
    else:
        for j in range(D_MODEL // 128):
            hf[j] = h[:, 128 * j:128 * (j + 1)]
        for r in range(d):
            for j in range(D_MODEL // 128):
                hperm[r * n:(r + 1) * n, 128 * j:128 * (j + 1)] = hf[j, pl.ds(r, n, stride=d), :].astype(BF16)

    CW = 256

    def project(col):
        return jnp.dot(hperm[...], w_ref[:, col:col + CW], preferred_element_type=F32)

    def store(col, res):
        for r in range(d):
            out_ref[r, :, col:col + res.shape[1]] = res[r * n:(r + 1) * n, :]

    lane = lax.broadcasted_iota(jnp.int32, (tm, 128), 1)
    low = (lane % HEAD_DIM) < (HEAD_DIM // 2)

    def rope(base, scale):
        for j in range(D_MODEL // CW):
            acc = project(base + CW * j)
            for k in range(CW // 128):
                xj = acc[:, 128 * k:128 * (k + 1)]
                partner = jnp.where(low, pltpu.roll(xj, 128 - HEAD_DIM // 2, 1), pltpu.roll(xj, HEAD_DIM // 2, 1))
                o = xj * cos_ref[...] + partner * sin_ref[...]
                if scale != 1.0:
                    o = o * scale
                store(base + CW * j + 128 * k, o.astype(BF16))

    rope(0, HEAD_DIM ** -0.5)
    rope(D_MODEL, 1.0)
    for j in range(D_MODEL // CW):
        store(2 * D_MODEL + CW * j, project(2 * D_MODEL + CW * j).astype(BF16))


def _qkv_call(x, adds, g, w, cos, sin, *, B, S, d, emit_x, tm=512):
    T = B * S
    nI = S // tm
    L = S // d
    n = tm // d
    n_add = len(adds)
    row = pl.BlockSpec((tm, D_MODEL), lambda i: (i, 0))
    in_specs = [row]
    T_rows = T // tm
    for k in range(n_add):
        in_specs.append(pl.BlockSpec((tm * LANE_CHUNKS, 128), lambda i, k=k: (k * T_rows + i, 0)))
    in_specs += [
        pl.BlockSpec((1, D_MODEL), lambda i: (0, 0)),
        pl.BlockSpec((D_MODEL, 3 * D_MODEL), lambda i: (0, 0)),
        pl.BlockSpec((tm, 128), lambda i: (i % nI, 0)),
        pl.BlockSpec((tm, 128), lambda i: (i % nI, 0)),
    ]
    out_spec = pl.BlockSpec((None, d, n, 3 * D_MODEL), lambda i: (i // nI, 0, i % nI, 0))
    out_shape = jax.ShapeDtypeStruct((B, d, L, 3 * D_MODEL), BF16)
    if emit_x:
        out_specs = [row, out_spec]
        out_shapes = [jax.ShapeDtypeStruct((T, D_MODEL), F32), out_shape]
    else:
        out_specs = [out_spec]
        out_shapes = [out_shape]
    res = pl.pallas_call(
        functools.partial(_qkv_kernel, n_add=n_add, d=d, tm=tm, emit_x=emit_x),
        grid=(T // tm,),
        in_specs=in_specs,
        out_specs=out_specs,
        out_shape=out_shapes,
        scratch_shapes=[pltpu.VMEM((tm, D_MODEL), BF16), pltpu.VMEM((D_MODEL // 128, tm, 128), F32)],
        compiler_params=_cparams(("parallel",)),
        name=f"qkv_d{d}",
    )(x, *adds, g, w, cos, sin)
    return res


def _attn_kernel(q_ref, kp_ref, kc_ref, kn_ref, vp_ref, vc_ref, vn_ref, o_ref, lse_ref, kbuf, vbuf, *, L):
    i = pl.program_id(1)
    W = Q_BLOCK + 2 * HALF
    kbuf[0:HALF, :] = kp_ref[...]
    kbuf[HALF:HALF + Q_BLOCK, :] = kc_ref[...]
    kbuf[HALF + Q_BLOCK:W, :] = kn_ref[...]
    vbuf[0:HALF, :] = vp_ref[...]
    vbuf[HALF:HALF + Q_BLOCK, :] = vc_ref[...]
    vbuf[HALF + Q_BLOCK:W, :] = vn_ref[...]
    QQ = 2 * Q_BLOCK
    row = lax.broadcasted_iota(jnp.int32, (QQ, W), 0) % Q_BLOCK
    col = lax.broadcasted_iota(jnp.int32, (QQ, W), 1)
    rel = col - row
    kpos = i * Q_BLOCK - HALF + col
    mask = (rel >= 0) & (rel <= 2 * HALF) & (kpos >= 0) & (kpos < L)
    lane = lax.broadcasted_iota(jnp.int32, (Q_BLOCK, 128), 1)
    first = lane < HEAD_DIM
    keep_a = first.astype(F32).astype(BF16)
    keep_b = 1 - keep_a
    lse_tile = jnp.zeros((Q_BLOCK, 128), F32)
    for j in range(N_HEADS // 2):
        sl = slice(128 * j, 128 * (j + 1))
        q2 = q_ref[:, sl]
        qq = jnp.concatenate([q2 * keep_a, q2 * keep_b], axis=0)
        s = lax.dot_general(qq, kbuf[:, sl], (((1,), (1,)), ((), ())), preferred_element_type=F32)
        s = jnp.where(mask, s, NEG_INF)
        m = jnp.max(s, axis=-1, keepdims=True)
        p = jnp.exp(s - m)
        l = jnp.sum(p, axis=-1, keepdims=True)
        o = jnp.dot(p.astype(BF16), vbuf[:, sl], preferred_element_type=F32) / l
        o_ref[:, sl] = jnp.where(first, o[:Q_BLOCK], o[Q_BLOCK:]).astype(BF16)
        lse = m + jnp.log(l)
        lse_tile = jnp.where(lane == 2 * j, lse[:Q_BLOCK],
                             jnp.where(lane == 2 * j + 1, lse[Q_BLOCK:], lse_tile))
    lse_ref[...] = lse_tile


def _attn_call(qkv, *, L):
    NS = qkv.shape[0]
    nq = L // Q_BLOCK
    nh = L // HALF
    q_spec = pl.BlockSpec((None, Q_BLOCK, D_MODEL), lambda s, i: (s, i, 0))

    def cur(cb):
        return pl.BlockSpec((None, Q_BLOCK, D_MODEL), lambda s, i: (s, i, cb))

    def prev(cb):
        return pl.BlockSpec((None, HALF, D_MODEL), lambda s, i: (s, jnp.maximum(2 * i - 1, 0), cb))

    def nxt(cb):
        return pl.BlockSpec((None, HALF, D_MODEL), lambda s, i: (s, jnp.minimum(2 * i + 2, nh - 1), cb))

    return pl.pallas_call(
        functools.partial(_attn_kernel, L=L),
        grid=(NS, nq),
        in_specs=[q_spec, prev(1), cur(1), nxt(1), prev(2), cur(2), nxt(2)],
        out_specs=[pl.BlockSpec((None, Q_BLOCK, D_MODEL), lambda s, i: (s, i, 0)),
                   pl.BlockSpec((None, Q_BLOCK, 128), lambda s, i: (s, i, 0))],
        out_shape=[jax.ShapeDtypeStruct((NS, L, D_MODEL), BF16),
                   jax.ShapeDtypeStruct((NS, L, 128), F32)],
        scratch_shapes=[pltpu.VMEM((Q_BLOCK + 2 * HALF, D_MODEL), BF16),
                        pltpu.VMEM((Q_BLOCK + 2 * HALF, D_MODEL), BF16)],
        compiler_params=_cparams(("parallel", "parallel")),
        name="band_attn",
    )(qkv, qkv, qkv, qkv, qkv, qkv, qkv)


def _post(x_new, fg_ref, rw_ref, rb_ref, xn_ref, h2_ref, rt_ref):
    xn_ref[...] = x_new
    h2 = _rms(x_new, fg_ref[...])
    _write_tile_rows(h2_ref, h2)
    lg = jnp.dot(h2.astype(BF16), rw_ref[...], preferred_element_type=F32) + rb_ref[...]
    lane = lax.broadcasted_iota(jnp.int32, lg.shape, 1)

    def first_argmax(v, vmax):
        return jnp.min(jnp.where(v == vmax, lane, ROUTER_LANES), axis=-1, keepdims=True)

    is_g = lane < N_GROUPS
    vg = jnp.where(is_g, lg, -jnp.inf)
    gmax = jnp.max(vg, axis=-1, keepdims=True)
    gi = first_argmax(vg, gmax)
    gp = 1.0 / jnp.sum(jnp.where(is_g, jnp.exp(lg - gmax), 0.0), axis=-1, keepdims=True)
    lo = N_GROUPS + EPG * gi
    in_grp = (lane >= lo) & (lane < lo + EPG)
    ve = jnp.where(in_grp, lg, -jnp.inf)
    m1 = jnp.max(ve, axis=-1, keepdims=True)
    i1 = first_argmax(ve, m1)
    ve2 = jnp.where(lane == i1, -jnp.inf, ve)
    m2 = jnp.max(ve2, axis=-1, keepdims=True)
    i2 = first_argmax(ve2, m2)
    esum = jnp.sum(jnp.where(in_grp, jnp.exp(lg - m1), 0.0), axis=-1, keepdims=True)
    ep1 = 1.0 / esum
    ep2 = jnp.exp(m2 - m1) / esum
    den = ep1 + ep2
    g1 = gp * ep1 / den
    g2 = gp * ep2 / den
    e1 = (i1 - N_GROUPS).astype(F32)
    e2 = (i2 - N_GROUPS).astype(F32)
    rt_ref[...] = jnp.where(lane == 0, e1, jnp.where(lane == 1, e2, jnp.where(lane == 2, g1,
                            jnp.where(lane == 3, g2, 0.0))))


def _oproj_kernel(o0, o1, o2, l0, l1, l2, wo_ref, x_ref, fg_ref, rw_ref, rb_ref,
                  xn_ref, h2_ref, lg_ref, onat, lnat, mbuf, *, tm):
    o_refs = (o0, o1, o2)
    l_refs = (l0, l1, l2)
    for g, d in enumerate(DILATIONS):
        n = tm // d
        for r in range(d):
            if d == 1:
                lnat[g] = l_refs[g][r]
                for j in range(D_MODEL // 128):
                    onat[g, j] = o_refs[g][r, :, 128 * j:128 * (j + 1)].astype(F32)
            else:
                lnat[g, pl.ds(r, n, stride=d), :] = l_refs[g][r]
                for j in range(D_MODEL // 128):
                    onat[g, j, pl.ds(r, n, stride=d), :] = o_refs[g][r, :, 128 * j:128 * (j + 1)].astype(F32)
    ls = [lnat[g] for g in range(3)]
    m = jnp.maximum(jnp.maximum(ls[0], ls[1]), ls[2])
    es = [jnp.exp(l - m) for l in ls]
    tot = es[0] + es[1] + es[2]
    al = [e / tot for e in es]
    lane = lax.broadcasted_iota(jnp.int32, (tm, 128), 1)
    first = lane < HEAD_DIM
    for j in range(D_MODEL // 128):
        acc = None
        for g in range(3):
            a = jnp.where(first, al[g][:, 2 * j:2 * j + 1], al[g][:, 2 * j + 1:2 * j + 2])
            t = a * onat[g, j]
            acc = t if acc is None else acc + t
        mbuf[:, 128 * j:128 * (j + 1)] = acc.astype(BF16)
    y = jnp.dot(mbuf[...], wo_ref[...], preferred_element_type=F32)
    _post(x_ref[...] + y, fg_ref, rw_ref, rb_ref, xn_ref, h2_ref, lg_ref)


def _oproj_call(os_, ls_, wo, x, fg, rw, rb, *, B, S, tm=256):
    T = B * S
    nI = S // tm
    in_specs = []
    for d in DILATIONS:
        in_specs.append(pl.BlockSpec((None, d, tm // d, D_MODEL), lambda i: (i // nI, 0, i % nI, 0)))
    for d in DILATIONS:
        in_specs.append(pl.BlockSpec((None, d, tm // d, 128), lambda i: (i // nI, 0, i % nI, 0)))
    row = pl.BlockSpec((tm, D_MODEL), lambda i: (i, 0))
    in_specs += [
        pl.BlockSpec((D_MODEL, D_MODEL), lambda i: (0, 0)),
        row,
        pl.BlockSpec((1, D_MODEL), lambda i: (0, 0)),
        pl.BlockSpec((D_MODEL, ROUTER_LANES), lambda i: (0, 0)),
        pl.BlockSpec((1, ROUTER_LANES), lambda i: (0, 0)),
    ]
    return pl.pallas_call(
        functools.partial(_oproj_kernel, tm=tm),
        grid=(T // tm,),
        in_specs=in_specs,
        out_specs=[row, pl.BlockSpec((tm * LANE_CHUNKS, 128), lambda i: (i, 0)),
                   pl.BlockSpec((tm, ROUTER_LANES), lambda i: (i, 0))],
        out_shape=[jax.ShapeDtypeStruct((T, D_MODEL), F32), jax.ShapeDtypeStruct((T * LANE_CHUNKS, 128), F32),
                   jax.ShapeDtypeStruct((T, ROUTER_LANES), F32)],
        scratch_shapes=[pltpu.VMEM((3, D_MODEL // 128, tm, 128), F32), pltpu.VMEM((3, tm, 128), F32),
                        pltpu.VMEM((tm, D_MODEL), BF16)],
        compiler_params=_cparams(("parallel",)),
        name="merge_oproj",
    )(*os_, *ls_, wo, x, fg, rw, rb)


def _pool_kernel(x_ref, y_ref, xp_ref, yp_ref, xq_ref, yq_ref,
                 g_ref, wp_ref, ps_ref, fg_ref, rw_ref, rb_ref, xn_ref, h2_ref, lg_ref, hp, xnew, *, tm, S):
    i = pl.program_id(0)
    nI = S // tm
    ii = i % nI
    g = g_ref[...]
    xs = x_ref[...] + _read_tile_rows(y_ref, tm)
    h = _rms(xs, g)
    hprev = _rms(xp_ref[...] + _read_tile_rows(yp_ref, HALO), g)
    hnext = _rms(xq_ref[...] + _read_tile_rows(yq_ref, HALO), g)
    hp[0:HALO, :] = jnp.where(ii > 0, hprev, 0.0)
    hp[HALO:HALO + tm, :] = h
    hp[HALO + tm:2 * HALO + tm, :] = jnp.where(ii < nI - 1, hnext, 0.0)
    t = ii * tm + lax.broadcasted_iota(jnp.int32, (tm, 1), 0)
    for k, w in enumerate(POOL_WINDOWS):
        cols = slice(k * POOL_CH, (k + 1) * POOL_CH)
        acc = None
        for j in range(-(w // 2), w // 2):
            v = hp[HALO + j:HALO + j + tm, cols]
            acc = v if acc is None else acc + v
        lo = jnp.clip(t - w // 2, 0, S)
        hi = jnp.clip(t - w // 2 + w, 0, S)
        mean = acc / (hi - lo).astype(F32)
        diff = (mean - h[:, cols]).astype(BF16)
        out = jnp.dot(diff, wp_ref[k], preferred_element_type=F32)
        xnew[:, cols] = xs[:, cols] + out * ps_ref[:, cols]
    _post(xnew[...], fg_ref, rw_ref, rb_ref, xn_ref, h2_ref, lg_ref)


def _pool_call(x, y2, g, wp, ps, fg, rw, rb, *, B, S, tm=512):
    T = B * S
    nb8 = T // HALO
    r8 = tm // HALO
    nT = T // tm

    def main(k, shape=(tm, D_MODEL)):
        return pl.BlockSpec(shape, lambda i: (k * nT + i, 0))

    def prev(k, shape=(HALO, D_MODEL)):
        return pl.BlockSpec(shape, lambda i: (k * nb8 + jnp.maximum(i * r8 - 1, 0), 0))

    def nxt(k, shape=(HALO, D_MODEL)):
        return pl.BlockSpec(shape, lambda i: (k * nb8 + jnp.minimum((i + 1) * r8, nb8 - 1), 0))

    row = pl.BlockSpec((tm, D_MODEL), lambda i: (i, 0))
    vec = pl.BlockSpec((1, D_MODEL), lambda i: (0, 0))
    tr_main = (tm * LANE_CHUNKS, 128)
    tr_halo = (HALO * LANE_CHUNKS, 128)
    in_specs = [main(0), main(0, tr_main), prev(0), prev(0, tr_halo), nxt(0), nxt(0, tr_halo),
                vec, pl.BlockSpec((len(POOL_WINDOWS), POOL_CH, POOL_CH), lambda i: (0, 0, 0)), vec, vec,
                pl.BlockSpec((D_MODEL, ROUTER_LANES), lambda i: (0, 0)),
                pl.BlockSpec((1, ROUTER_LANES), lambda i: (0, 0))]
    return pl.pallas_call(
        functools.partial(_pool_kernel, tm=tm, S=S),
        grid=(nT,),
        in_specs=in_specs,
        out_specs=[row, pl.BlockSpec(tr_main, lambda i: (i, 0)),
                   pl.BlockSpec((tm, ROUTER_LANES), lambda i: (i, 0))],
        out_shape=[jax.ShapeDtypeStruct((T, D_MODEL), F32), jax.ShapeDtypeStruct((T * LANE_CHUNKS, 128), F32),
                   jax.ShapeDtypeStruct((T, ROUTER_LANES), F32)],
        scratch_shapes=[pltpu.VMEM((tm + 2 * HALO, D_MODEL), F32), pltpu.VMEM((tm, D_MODEL), F32)],
        compiler_params=_cparams(("parallel",)),
        name="pool_mixer",
    )(x, y2, x, y2, x, y2, g, wp, ps, fg, rw, rb)


def _moe_kernel(vt_ref, vea_ref, veb_ref, vlo_ref, vhi_ref, tokc_ref, tokn_ref, ga_ref, gb_ref, h_hbm,
                wga_ref, wua_ref, wda_ref, wgb_ref, wub_ref, wdb_ref, y_hbm, xbuf, ybuf, gsem, ssem, *, T):
    del vea_ref, veb_ref
    i = pl.program_id(0)
    nb = pl.num_programs(0)
    slot = i % 2
    base = vt_ref[i] * MOE_TILE
    lo = vlo_ref[i]
    hi = vhi_ref[i]
    junk = T + slot * MOE_TILE
    RC = LANE_CHUNKS

    def row_in(tok, s, r):
        return pltpu.make_async_copy(h_hbm.at[pl.ds(pl.multiple_of(tok * RC, RC), RC), :],
                                     xbuf.at[s, pl.ds(r * RC, RC), :], gsem.at[s])

    def row_out(row, s, r):
        return pltpu.make_async_copy(ybuf.at[s, pl.ds(r * RC, RC), :],
                                     y_hbm.at[pl.ds(pl.multiple_of(row * RC, RC), RC), :], ssem.at[s])

    def gather(tok_ref, s):
        for r in range(MOE_TILE):
            row_in(tok_ref[0, r], s, r).start(priority=r % 2)

    def gather_wait(s):
        for r in range(MOE_TILE):
            row_in(0, s, r).wait()

    def scatter_wait(s):
        for r in range(MOE_TILE):
            row_out(0, s, r).wait()

    @pl.when(i == 0)
    def _():
        gather(tokc_ref, 0)

    @pl.when(i + 1 < nb)
    def _():
        gather(tokn_ref, 1 - slot)

    gather_wait(slot)

    @pl.when(i >= 2)
    def _():
        scatter_wait(slot)

    x = _read_tile_rows(xbuf.at[slot], MOE_TILE).astype(BF16)

    def expert(wg_ref, wu_ref, wd_ref):
        g = jnp.dot(x, wg_ref[...], preferred_element_type=F32)
        u = jnp.dot(x, wu_ref[...], preferred_element_type=F32)
        a = (g * (1.0 / (1.0 + jnp.exp(-g)))) * u
        return jnp.dot(a.astype(BF16), wd_ref[...], preferred_element_type=F32)

    y = expert(wga_ref, wua_ref, wda_ref) * ga_ref[...] + expert(wgb_ref, wub_ref, wdb_ref) * gb_ref[...]
    _write_tile_rows(ybuf.at[slot], y)

    whole = (base >= lo) & (base + MOE_TILE <= hi)

    @pl.when(whole)
    def _():
        for r in range(MOE_TILE):
            row_out(tokc_ref[0, r], slot, r).start(priority=r % 2)

    @pl.when(jnp.logical_not(whole))
    def _():
        for r in range(MOE_TILE):
            mine = (base + r >= lo) & (base + r < hi)
            row_out(jnp.where(mine, tokc_ref[0, r], junk + r), slot, r).start(priority=r % 2)

    @pl.when(i == nb - 1)
    def _():
        scatter_wait(slot)

        @pl.when(nb >= 2)
        def _():
            scatter_wait(1 - slot)


def _moe_call(h2, plan, wg, wu, wd, *, T, layer):
    vt, vea, veb, vlo, vhi, tok, ga, gb = plan
    nb = vt.shape[0]
    smem_row = functools.partial(pl.BlockSpec, (None, 1, MOE_TILE), memory_space=pltpu.SMEM)
    buf = pltpu.VMEM((2, MOE_TILE * LANE_CHUNKS, 128), F32)
    w_in = pl.BlockSpec((None, None, D_MODEL, D_EXPERT), lambda i, vt, ea, eb, lo, hi: (layer, ea[i], 0, 0))
    w_in_b = pl.BlockSpec((None, None, D_MODEL, D_EXPERT), lambda i, vt, ea, eb, lo, hi: (layer, eb[i], 0, 0))
    w_out = pl.BlockSpec((None, None, D_EXPERT, D_MODEL), lambda i, vt, ea, eb, lo, hi: (layer, ea[i], 0, 0))
    w_out_b = pl.BlockSpec((None, None, D_EXPERT, D_MODEL), lambda i, vt, ea, eb, lo, hi: (layer, eb[i], 0, 0))
    gate_spec = pl.BlockSpec((MOE_TILE, 1), lambda i, vt, ea, eb, lo, hi: (vt[i], 0))
    grid_spec = pltpu.PrefetchScalarGridSpec(
        num_scalar_prefetch=5,
        grid=(nb,),
        in_specs=[
            smem_row(lambda i, vt, ea, eb, lo, hi: (vt[i], 0, 0)),
            smem_row(lambda i, vt, ea, eb, lo, hi: (vt[jnp.minimum(i + 1, nb - 1)], 0, 0)),
            gate_spec, gate_spec,
            pl.BlockSpec(memory_space=pl.ANY),
            w_in, w_in, w_out, w_in_b, w_in_b, w_out_b,
        ],
        out_specs=pl.BlockSpec(memory_space=pl.ANY),
        scratch_shapes=[buf, buf, pltpu.SemaphoreType.DMA((2,)), pltpu.SemaphoreType.DMA((2,))],
    )
    return pl.pallas_call(
        functools.partial(_moe_kernel, T=T),
        grid_spec=grid_spec,
        out_shape=jax.ShapeDtypeStruct(((T + 2 * MOE_TILE) * LANE_CHUNKS, 128), F32),
        compiler_params=_cparams(("arbitrary",)),
        name="moe_experts",
    )(vt, vea, veb, vlo, vhi, tok, tok, ga, gb, h2, wg, wu, wd, wg, wu, wd)


def _final_kernel(x_ref, y_ref, g_ref, o_ref, *, tm):
    o_ref[...] = _rms(x_ref[...] + _read_tile_rows(y_ref, tm), g_ref[...])


def _final_call(x, y2, g, *, T, tm=512):
    nT = T // tm
    return pl.pallas_call(
        functools.partial(_final_kernel, tm=tm),
        grid=(nT,),
        in_specs=[pl.BlockSpec((tm, D_MODEL), lambda i: (i, 0)),
                  pl.BlockSpec((tm * LANE_CHUNKS, 128), lambda i: (i, 0)),
                  pl.BlockSpec((1, D_MODEL), lambda i: (0, 0))],
        out_specs=pl.BlockSpec((tm, D_MODEL), lambda i: (i, 0)),
        out_shape=jax.ShapeDtypeStruct((T, D_MODEL), F32),
        compiler_params=_cparams(("parallel",)),
        name="final_norm",
    )(x, y2, g)


def _plan(route, T):
    n_tiles = T // MOE_TILE
    n_cls = N_GROUPS * EPG * EPG
    nb = n_tiles + N_GROUPS * (EPG * (EPG - 1) // 2) - 1
    e1 = route[:, 0].astype(jnp.int32)
    e2 = route[:, 1].astype(jnp.int32)
    swap = e2 < e1
    ea, eb = jnp.where(swap, e2, e1), jnp.where(swap, e1, e2)
    ga, gb = jnp.where(swap, route[:, 3], route[:, 2]), jnp.where(swap, route[:, 2], route[:, 3])
    cls = (ea // EPG) * (EPG * EPG) + (ea % EPG) * EPG + eb % EPG
    _, tok_s, ga_s, gb_s = lax.sort((cls, jnp.arange(T, dtype=jnp.int32), ga, gb), num_keys=1, is_stable=True)
    cids = jnp.arange(n_cls, dtype=jnp.int32)
    counts = jnp.sum(cls[:, None] == cids[None, :], axis=0, dtype=jnp.int32)
    ends = jnp.cumsum(counts)
    starts = ends - counts
    first_tile = starts // MOE_TILE
    nvis = jnp.where(counts > 0, (ends - 1) // MOE_TILE - first_tile + 1, 0)
    vend = jnp.cumsum(nvis)
    vstart = vend - nvis
    v = jnp.arange(nb, dtype=jnp.int32)
    vc = jnp.minimum(jnp.sum(v[:, None] >= vend[None, :], axis=1, dtype=jnp.int32), n_cls - 1)
    onehot = (vc[:, None] == cids[None, :]).astype(jnp.int32)
    pick = lambda tab: jnp.sum(onehot * tab[None, :], axis=1, dtype=jnp.int32)
    live = v < vend[n_cls - 1]
    vt = jnp.where(live, pick(first_tile) + v - pick(vstart), n_tiles - 1)
    vlo = jnp.where(live, pick(starts), 0)
    vhi = jnp.where(live, pick(ends), 0)
    grp = vc // (EPG * EPG)
    vea = grp * EPG + (vc // EPG) % EPG
    veb = grp * EPG + vc % EPG
    return (vt, vea, veb, vlo, vhi, tok_s.reshape(n_tiles, 1, MOE_TILE), ga_s.reshape(T, 1), gb_s.reshape(T, 1))


def _rope_tables(S, d, tm):
    n = tm // d
    j = jnp.arange(S, dtype=jnp.int32)
    tile, rem = j // tm, j % tm
    pos = ((tile * n + rem % n) * d + rem // n).astype(F32)
    inv_freq = 1.0 / (ROPE_THETA ** (jnp.arange(0, HEAD_DIM, 2, dtype=F32) / HEAD_DIM))
    ang = pos[:, None] * inv_freq[None, :]
    cos, sin = jnp.cos(ang), jnp.sin(ang)
    return jnp.tile(cos, (1, 4)), jnp.concatenate([-sin, sin, -sin, sin], axis=1)


def _trunk(x, p, *, B, S):
    T = B * S
    x = x.reshape(T, D_MODEL)
    tm_qkv = 512
    tables = [_rope_tables(S, d, tm_qkv) for d in DILATIONS]
    y2 = None
    for i in range(DEPTH):
        gm = p["mixer_norm"][i][None, :]
        fg = p["ffn_norm"][i][None, :]
        rw, rb = p["router_w"][i], p["router_b"][i]
        if i % 2 == 0:
            li = i // 2
            os_, ls_ = [], []
            adds = [] if y2 is None else [y2]
            xs = x
            for gi_, d in enumerate(DILATIONS):
                emit = bool(adds) and gi_ == 0
                res = _qkv_call(x, adds, gm, p["w_qkv"][li][gi_], tables[gi_][0], tables[gi_][1],
                                B=B, S=S, d=d, emit_x=emit, tm=tm_qkv)
                if emit:
                    xs, qkv = res
                else:
                    (qkv,) = res
                L = S // d
                o, lse = _attn_call(qkv.reshape(B * d, L, 3 * D_MODEL), L=L)
                os_.append(o.reshape(B, d, L, D_MODEL))
                ls_.append(lse.reshape(B, d, L, 128))
            x, h2, route = _oproj_call(os_, ls_, p["w_o"][li], xs, fg, rw, rb, B=B, S=S)
        else:
            li = i // 2
            x, h2, route = _pool_call(x, y2, gm, p["w_pool"][li], p["pool_scale"][li][None, :], fg, rw, rb,
                                      B=B, S=S)
        y2 = _moe_call(h2, _plan(route, T), p["w_gate"], p["w_up"], p["w_down"], T=T, layer=i)
    out = _final_call(x, y2, p["final_norm"][None, :], T=T)
    return out.reshape(B, S, D_MODEL)


def kernel(x_prompt, x_sample, mixer_norm, w_qkv, w_o, w_pool, pool_scale, ffn_norm, router_group_w,
           router_group_b, router_expert_w, router_expert_b, w_gate, w_up, w_down, final_norm):
    n_attn = w_qkv.shape[0]
    wq = w_qkv.astype(BF16).reshape(n_attn, D_MODEL, len(DILATIONS), 3 * D_MODEL).transpose(0, 2, 1, 3)
    pad = ROUTER_LANES - N_GROUPS - N_EXPERTS
    router_w = jnp.pad(jnp.concatenate([router_group_w, router_expert_w], axis=-1),
                       ((0, 0), (0, 0), (0, pad))).astype(BF16)
    router_b = jnp.pad(jnp.concatenate([router_group_b, router_expert_b], axis=-1), ((0, 0), (0, pad)))[:, None, :]
    p = dict(mixer_norm=mixer_norm, ffn_norm=ffn_norm, final_norm=final_norm, pool_scale=pool_scale,
             w_qkv=wq, w_o=w_o.astype(BF16), w_pool=w_pool.astype(BF16),
             router_w=router_w, router_b=router_b,
             w_gate=w_gate.astype(BF16), w_up=w_up.astype(BF16), w_down=w_down.astype(BF16))
    y_prompt = _trunk(x_prompt, p, B=x_prompt.shape[0], S=x_prompt.shape[1])
    y_sample = _trunk(x_sample, p, B=x_sample.shape[0], S=x_sample.shape[1])
    return (y_prompt, y_sample)
```

```python
import functools

import jax
import jax.numpy as jnp
from jax import lax
from jax.experimental import pallas as pl
from jax.experimental.pallas import tpu as pltpu

F32 = jnp.float32
BF16 = jnp.bfloat16

D_MODEL = 1024
DEPTH = 4
DILATIONS = (1, 4, 16)
HALF = 64
N_HEADS = 16
HEAD_DIM = 64
Q_BLOCK = 128
ROPE_THETA = 10000.0
NEG_INF = -1e30
POOL_WINDOWS = (2, 4, 8, 16)
POOL_CH = 256
N_GROUPS = 4
EPG = 8
N_EXPERTS = 32
TOP_K = 2
D_EXPERT = 512
MOE_TILE = 512
RMS_EPS = 1e-6
ROUTER_LANES = 128
HALO = 8
VMEM_LIMIT = 48 * 1024 * 1024


def _rms(x, g):
    return x * lax.rsqrt(jnp.mean(x * x, axis=-1, keepdims=True) + RMS_EPS) * g


def _cparams(sem):
    return pltpu.CompilerParams(dimension_semantics=sem, vmem_limit_bytes=VMEM_LIMIT)


LANE_CHUNKS = D_MODEL // 128


def _read_tile_rows(ref, n, start=0):
    return jnp.concatenate([ref[pl.ds(start * LANE_CHUNKS + j, n, stride=LANE_CHUNKS), :]
                            for j in range(LANE_CHUNKS)], axis=1)


def _write_tile_rows(ref, val, start=0):
    n = val.shape[0]
    for j in range(LANE_CHUNKS):
        ref[pl.ds(start * LANE_CHUNKS + j, n, stride=LANE_CHUNKS), :] = val[:, 128 * j:128 * (j + 1)]


def _qkv_kernel(*refs, n_add, d, tm, emit_x):
    x_ref = refs[0]
    add_refs = refs[1:1 + n_add]
    g_ref, w_ref, cos_ref, sin_ref = refs[1 + n_add:5 + n_add]
    outs = refs[5 + n_add:]
    if emit_x:
        xs_ref, out_ref, hperm, hf = outs
    else:
        out_ref, hperm, hf = outs
    n = tm // d

    xs = x_ref[...]
    for a in add_refs:
        xs = xs + _read_tile_rows(a, tm)
    if emit_x:
        xs_ref[...] = xs
    h = _rms(xs, g_ref[...])
    if d == 1:
        hperm[...] = h.astype(BF16)
    else:
        for j in range(D_MODEL // 128):
            hf[j] = h[:, 128 * j:128 * (j + 1)]
        for r in range(d):
            for j in range(D_MODEL // 128):
                hperm[r * n:(r + 1) * n, 128 * j:128 * (j + 1)] = hf[j, pl.ds(r, n, stride=d), :].astype(BF16)

    CW = 256

    def project(col):
        return jnp.dot(hperm[...], w_ref[:, col:col + CW], preferred_element_type=F32)

    def store(col, res):
        for r in range(d):
            out_ref[r, :, col:col + res.shape[1]] = res[r * n:(r + 1) * n, :]

    lane = lax.broadcasted_iota(jnp.int32, (tm, 128), 1)
    low = (lane % HEAD_DIM) < (HEAD_DIM // 2)

    def rope(base, scale):
        for j in range(D_MODEL // CW):
            acc = project(base + CW * j)
            for k in range(CW // 128):
                xj = acc[:, 128 * k:128 * (k + 1)]
                partner = jnp.where(low, pltpu.roll(xj, 128 - HEAD_DIM // 2, 1), pltpu.roll(xj, HEAD_DIM // 2, 1))
                o = xj * cos_ref[...] + partner * sin_ref[...]
                if scale != 1.0:
                    o = o * scale
                store(base + CW * j + 128 * k, o.astype(BF16))

    rope(0, HEAD_DIM ** -0.5)
    rope(D_MODEL, 1.0)
    for j in range(D_MODEL // CW):
        store(2 * D_MODEL + CW * j, project(2 * D_MODEL + CW * j).astype(BF16))


def _qkv_call(x, adds, g, w, cos, sin, *, B, S, d, emit_x, tm=512):
    T = B * S
    nI = S // tm
    L = S // d
    n = tm // d
    n_add = len(adds)
    row = pl.BlockSpec((tm, D_MODEL), lambda i: (i, 0))
    in_specs = [row]
    T_rows = T // tm
    for k in range(n_add):
        in_specs.append(pl.BlockSpec((tm * LANE_CHUNKS, 128), lambda i, k=k: (k * T_rows + i, 0)))
    in_specs += [
        pl.BlockSpec((1, D_MODEL), lambda i: (0, 0)),
        pl.BlockSpec((D_MODEL, 3 * D_MODEL), lambda i: (0, 0)),
        pl.BlockSpec((tm, 128), lambda i: (i % nI, 0)),
        pl.BlockSpec((tm, 128), lambda i: (i % nI, 0)),
    ]
    out_spec = pl.BlockSpec((None, d, n, 3 * D_MODEL), lambda i: (i // nI, 0, i % nI, 0))
    out_shape = jax.ShapeDtypeStruct((B, d, L, 3 * D_MODEL), BF16)
    if emit_x:
        out_specs = [row, out_spec]
        out_shapes = [jax.ShapeDtypeStruct((T, D_MODEL), F32), out_shape]
    else:
        out_specs = [out_spec]
        out_shapes = [out_shape]
    res = pl.pallas_call(
        functools.partial(_qkv_kernel, n_add=n_add, d=d, tm=tm, emit_x=emit_x),
        grid=(T // tm,),
        in_specs=in_specs,
        out_specs=out_specs,
        out_shape=out_shapes,
        scratch_shapes=[pltpu.VMEM((tm, D_MODEL), BF16), pltpu.VMEM((D_MODEL // 128, tm, 128), F32)],
        compiler_params=_cparams(("parallel",)),
        name=f"qkv_d{d}",
    )(x, *adds, g, w, cos, sin)
    return res


def _attn_kernel(q_ref, kp_ref, kc_ref, kn_ref, vp_ref, vc_ref, vn_ref, o_ref, lse_ref, kbuf, vbuf, *, L):
    i = pl.program_id(1)
    W = Q_BLOCK + 2 * HALF
    kbuf[0:HALF, :] = kp_ref[...]
    kbuf[HALF:HALF + Q_BLOCK, :] = kc_ref[...]
    kbuf[HALF + Q_BLOCK:W, :] = kn_ref[...]
    vbuf[0:HALF, :] = vp_ref[...]
    vbuf[HALF:HALF + Q_BLOCK, :] = vc_ref[...]
    vbuf[HALF + Q_BLOCK:W, :] = vn_ref[...]
    QQ = 2 * Q_BLOCK
    row = lax.broadcasted_iota(jnp.int32, (QQ, W), 0) % Q_BLOCK
    col = lax.broadcasted_iota(jnp.int32, (QQ, W), 1)
    rel = col - row
    kpos = i * Q_BLOCK - HALF + col
    mask = (rel >= 0) & (rel <= 2 * HALF) & (kpos >= 0) & (kpos < L)
    lane = lax.broadcasted_iota(jnp.int32, (Q_BLOCK, 128), 1)
    first = lane < HEAD_DIM
    keep_a = first.astype(F32).astype(BF16)
    keep_b = 1 - keep_a
    lse_tile = jnp.zeros((Q_BLOCK, 128), F32)
    for j in range(N_HEADS // 2):
        sl = slice(128 * j, 128 * (j + 1))
        q2 = q_ref[:, sl]
        qq = jnp.concatenate([q2 * keep_a, q2 * keep_b], axis=0)
        s = lax.dot_general(qq, kbuf[:, sl], (((1,), (1,)), ((), ())), preferred_element_type=F32)
        s = jnp.where(mask, s, NEG_INF)
        m = jnp.max(s, axis=-1, keepdims=True)
        p = jnp.exp(s - m)
        l = jnp.sum(p, axis=-1, keepdims=True)
        o = jnp.dot(p.astype(BF16), vbuf[:, sl], preferred_element_type=F32) / l
        o_ref[:, sl] = jnp.where(first, o[:Q_BLOCK], o[Q_BLOCK:]).astype(BF16)
        lse = m + jnp.log(l)
        lse_tile = jnp.where(lane == 2 * j, lse[:Q_BLOCK],
                             jnp.where(lane == 2 * j + 1, lse[Q_BLOCK:], lse_tile))
    lse_ref[...] = lse_tile


def _attn_call(qkv, *, L):
    NS = qkv.shape[0]
    nq = L // Q_BLOCK
    nh = L // HALF
    q_spec = pl.BlockSpec((None, Q_BLOCK, D_MODEL), lambda s, i: (s, i, 0))

    def cur(cb):
        return pl.BlockSpec((None, Q_BLOCK, D_MODEL), lambda s, i: (s, i, cb))

    def prev(cb):
        return pl.BlockSpec((None, HALF, D_MODEL), lambda s, i: (s, jnp.maximum(2 * i - 1, 0), cb))

    def nxt(cb):
        return pl.BlockSpec((None, HALF, D_MODEL), lambda s, i: (s, jnp.minimum(2 * i + 2, nh - 1), cb))

    return pl.pallas_call(
        functools.partial(_attn_kernel, L=L),
        grid=(NS, nq),
        in_specs=[q_spec, prev(1), cur(1), nxt(1), prev(2), cur(2), nxt(2)],
        out_specs=[pl.BlockSpec((None, Q_BLOCK, D_MODEL), lambda s, i: (s, i, 0)),
                   pl.BlockSpec((None, Q_BLOCK, 128), lambda s, i: (s, i, 0))],
        out_shape=[jax.ShapeDtypeStruct((NS, L, D_MODEL), BF16),
                   jax.ShapeDtypeStruct((NS, L, 128), F32)],
        scratch_shapes=[pltpu.VMEM((Q_BLOCK + 2 * HALF, D_MODEL), BF16),
                        pltpu.VMEM((Q_BLOCK + 2 * HALF, D_MODEL), BF16)],
        compiler_params=_cparams(("parallel", "parallel")),
        name="band_attn",
    )(qkv, qkv, qkv, qkv, qkv, qkv, qkv)


def _post(x_new, fg_ref, rw_ref, rb_ref, xn_ref, h2_ref, rt_ref):
    xn_ref[...] = x_new
    h2 = _rms(x_new, fg_ref[...])
    _write_tile_rows(h2_ref, h2)
    lg = jnp.dot(h2.astype(BF16), rw_ref[...], preferred_element_type=F32) + rb_ref[...]
    lane = lax.broadcasted_iota(jnp.int32, lg.shape, 1)

    def first_argmax(v, vmax):
        return jnp.min(jnp.where(v == vmax, lane, ROUTER_LANES), axis=-1, keepdims=True)

    is_g = lane < N_GROUPS
    vg = jnp.where(is_g, lg, -jnp.inf)
    gmax = jnp.max(vg, axis=-1, keepdims=True)
    gi = first_argmax(vg, gmax)
    gp = 1.0 / jnp.sum(jnp.where(is_g, jnp.exp(lg - gmax), 0.0), axis=-1, keepdims=True)
    lo = N_GROUPS + EPG * gi
    in_grp = (lane >= lo) & (lane < lo + EPG)
    ve = jnp.where(in_grp, lg, -jnp.inf)
    m1 = jnp.max(ve, axis=-1, keepdims=True)
    i1 = first_argmax(ve, m1)
    ve2 = jnp.where(lane == i1, -jnp.inf, ve)
    m2 = jnp.max(ve2, axis=-1, keepdims=True)
    i2 = first_argmax(ve2, m2)
    esum = jnp.sum(jnp.where(in_grp, jnp.exp(lg - m1), 0.0), axis=-1, keepdims=True)
    ep1 = 1.0 / esum
    ep2 = jnp.exp(m2 - m1) / esum
    den = ep1 + ep2
    g1 = gp * ep1 / den
    g2 = gp * ep2 / den
    e1 = (i1 - N_GROUPS).astype(F32)
    e2 = (i2 - N_GROUPS).astype(F32)
    rt_ref[...] = jnp.where(lane == 0, e1, jnp.where(lane == 1, e2, jnp.where(lane == 2, g1,
                            jnp.where(lane == 3, g2, 0.0))))


def _oproj_kernel(o0, o1, o2, l0, l1, l2, wo_ref, x_ref, fg_ref, rw_ref, rb_ref,
                  xn_ref, h2_ref, lg_ref, onat, lnat, mbuf, *, tm):
    o_refs = (o0, o1, o2)
    l_refs = (l0, l1, l2)
    for g, d in enumerate(DILATIONS):
        n = tm // d
        for r in range(d):
            if d == 1:
                lnat[g] = l_refs[g][r]
                for j in range(D_MODEL // 128):
                    onat[g, j] = o_refs[g][r, :, 128 * j:128 * (j + 1)].astype(F32)
            else:
                lnat[g, pl.ds(r, n, stride=d), :] = l_refs[g][r]
                for j in range(D_MODEL // 128):
                    onat[g, j, pl.ds(r, n, stride=d), :] = o_refs[g][r, :, 128 * j:128 * (j + 1)].astype(F32)
    ls = [lnat[g] for g in range(3)]
    m = jnp.maximum(jnp.maximum(ls[0], ls[1]), ls[2])
    es = [jnp.exp(l - m) for l in ls]
    tot = es[0] + es[1] + es[2]
    al = [e / tot for e in es]
    lane = lax.broadcasted_iota(jnp.int32, (tm, 128), 1)
    first = lane < HEAD_DIM
    for j in range(D_MODEL // 128):
        acc = None
        for g in range(3):
            a = jnp.where(first, al[g][:, 2 * j:2 * j + 1], al[g][:, 2 * j + 1:2 * j + 2])
            t = a * onat[g, j]
            acc = t if acc is None else acc + t
        mbuf[:, 128 * j:128 * (j + 1)] = acc.astype(BF16)
    y = jnp.dot(mbuf[...], wo_ref[...], preferred_element_type=F32)
    _post(x_ref[...] + y, fg_ref, rw_ref, rb_ref, xn_ref, h2_ref, lg_ref)


def _oproj_call(os_, ls_, wo, x, fg, rw, rb, *, B, S, tm=256):
    T = B * S
    nI = S // tm
    in_specs = []
    for d in DILATIONS:
        in_specs.append(pl.BlockSpec((None, d, tm // d, D_MODEL), lambda i: (i // nI, 0, i % nI, 0)))
    for d in DILATIONS:
        in_specs.append(pl.BlockSpec((None, d, tm // d, 128), lambda i: (i // nI, 0, i % nI, 0)))
    row = pl.BlockSpec((tm, D_MODEL), lambda i: (i, 0))
    in_specs += [
        pl.BlockSpec((D_MODEL, D_MODEL), lambda i: (0, 0)),
        row,
        pl.BlockSpec((1, D_MODEL), lambda i: (0, 0)),
        pl.BlockSpec((D_MODEL, ROUTER_LANES), lambda i: (0, 0)),
        pl.BlockSpec((1, ROUTER_LANES), lambda i: (0, 0)),
    ]
    return pl.pallas_call(
        functools.partial(_oproj_kernel, tm=tm),
        grid=(T // tm,),
        in_specs=in_specs,
        out_specs=[row, pl.BlockSpec((tm * LANE_CHUNKS, 128), lambda i: (i, 0)),
                   pl.BlockSpec((tm, ROUTER_LANES), lambda i: (i, 0))],
        out_shape=[jax.ShapeDtypeStruct((T, D_MODEL), F32), jax.ShapeDtypeStruct((T * LANE_CHUNKS, 128), F32),
                   jax.ShapeDtypeStruct((T, ROUTER_LANES), F32)],
        scratch_shapes=[pltpu.VMEM((3, D_MODEL // 128, tm, 128), F32), pltpu.VMEM((3, tm, 128), F32),
                        pltpu.VMEM((tm, D_MODEL), BF16)],
        compiler_params=_cparams(("parallel",)),
        name="merge_oproj",
    )(*os_, *ls_, wo, x, fg, rw, rb)


def _pool_kernel(x_ref, ya_ref, yb_ref, xp_ref, yap_ref, ybp_ref, xq_ref, yaq_ref, ybq_ref,
                 g_ref, wp_ref, ps_ref, fg_ref, rw_ref, rb_ref, xn_ref, h2_ref, lg_ref, hp, xnew, *, tm, S):
    i = pl.program_id(0)
    nI = S // tm
    ii = i % nI
    g = g_ref[...]
    xs = x_ref[...] + _read_tile_rows(ya_ref, tm) + _read_tile_rows(yb_ref, tm)
    h = _rms(xs, g)
    hprev = _rms(xp_ref[...] + _read_tile_rows(yap_ref, HALO) + _read_tile_rows(ybp_ref, HALO), g)
    hnext = _rms(xq_ref[...] + _read_tile_rows(yaq_ref, HALO) + _read_tile_rows(ybq_ref, HALO), g)
    hp[0:HALO, :] = jnp.where(ii > 0, hprev, 0.0)
    hp[HALO:HALO + tm, :] = h
    hp[HALO + tm:2 * HALO + tm, :] = jnp.where(ii < nI - 1, hnext, 0.0)
    t = ii * tm + lax.broadcasted_iota(jnp.int32, (tm, 1), 0)
    for k, w in enumerate(POOL_WINDOWS):
        cols = slice(k * POOL_CH, (k + 1) * POOL_CH)
        acc = None
        for j in range(-(w // 2), w // 2):
            v = hp[HALO + j:HALO + j + tm, cols]
            acc = v if acc is None else acc + v
        lo = jnp.clip(t - w // 2, 0, S)
        hi = jnp.clip(t - w // 2 + w, 0, S)
        mean = acc / (hi - lo).astype(F32)
        diff = (mean - h[:, cols]).astype(BF16)
        out = jnp.dot(diff, wp_ref[k], preferred_element_type=F32)
        xnew[:, cols] = xs[:, cols] + out * ps_ref[:, cols]
    _post(xnew[...], fg_ref, rw_ref, rb_ref, xn_ref, h2_ref, lg_ref)


def _pool_call(x, y2, g, wp, ps, fg, rw, rb, *, B, S, tm=512):
    T = B * S
    nb8 = T // HALO
    r8 = tm // HALO
    nT = T // tm

    def main(k, shape=(tm, D_MODEL)):
        return pl.BlockSpec(shape, lambda i: (k * nT + i, 0))

    def prev(k, shape=(HALO, D_MODEL)):
        return pl.BlockSpec(shape, lambda i: (k * nb8 + jnp.maximum(i * r8 - 1, 0), 0))

    def nxt(k, shape=(HALO, D_MODEL)):
        return pl.BlockSpec(shape, lambda i: (k * nb8 + jnp.minimum((i + 1) * r8, nb8 - 1), 0))

    row = pl.BlockSpec((tm, D_MODEL), lambda i: (i, 0))
    vec = pl.BlockSpec((1, D_MODEL), lambda i: (0, 0))
    tr_main = (tm * LANE_CHUNKS, 128)
    tr_halo = (HALO * LANE_CHUNKS, 128)
    in_specs = [main(0), main(0, tr_main), main(1, tr_main), prev(0), prev(0, tr_halo), prev(1, tr_halo),
                nxt(0), nxt(0, tr_halo), nxt(1, tr_halo),
                vec, pl.BlockSpec((len(POOL_WINDOWS), POOL_CH, POOL_CH), lambda i: (0, 0, 0)), vec, vec,
                pl.BlockSpec((D_MODEL, ROUTER_LANES), lambda i: (0, 0)),
                pl.BlockSpec((1, ROUTER_LANES), lambda i: (0, 0))]
    return pl.pallas_call(
        functools.partial(_pool_kernel, tm=tm, S=S),
        grid=(nT,),
        in_specs=in_specs,
        out_specs=[row, pl.BlockSpec(tr_main, lambda i: (i, 0)),
                   pl.BlockSpec((tm, ROUTER_LANES), lambda i: (i, 0))],
        out_shape=[jax.ShapeDtypeStruct((T, D_MODEL), F32), jax.ShapeDtypeStruct((T * LANE_CHUNKS, 128), F32),
                   jax.ShapeDtypeStruct((T, ROUTER_LANES), F32)],
        scratch_shapes=[pltpu.VMEM((tm + 2 * HALO, D_MODEL), F32), pltpu.VMEM((tm, D_MODEL), F32)],
        compiler_params=_cparams(("parallel",)),
        name="pool_mixer",
    )(x, y2, y2, x, y2, y2, x, y2, y2, g, wp, ps, fg, rw, rb)


def _moe_kernel(vt_ref, ve_ref, vlo_ref, vhi_ref, tokc_ref, tokn_ref, dst_ref, gate_ref, h_hbm,
                wg_ref, wu_ref, wd_ref, y_hbm, xbuf, ybuf, gsem, ssem, *, T):
    del ve_ref
    i = pl.program_id(0)
    nb = pl.num_programs(0)
    slot = i % 2
    base = vt_ref[i] * MOE_TILE
    lo = vlo_ref[i]
    hi = vhi_ref[i]
    junk = 2 * T + slot * MOE_TILE
    RC = LANE_CHUNKS

    def row_in(tok, s, r):
        return pltpu.make_async_copy(h_hbm.at[pl.ds(pl.multiple_of(tok * RC, RC), RC), :],
                                     xbuf.at[s, pl.ds(r * RC, RC), :], gsem.at[s])

    def row_out(row, s, r):
        return pltpu.make_async_copy(ybuf.at[s, pl.ds(r * RC, RC), :],
                                     y_hbm.at[pl.ds(pl.multiple_of(row * RC, RC), RC), :], ssem.at[s])

    def gather(tok_ref, s):
        for r in range(MOE_TILE):
            row_in(tok_ref[0, r], s, r).start(priority=r % 2)

    def gather_wait(s):
        for r in range(MOE_TILE):
            row_in(0, s, r).wait()

    def scatter_wait(s):
        for r in range(MOE_TILE):
            row_out(0, s, r).wait()

    @pl.when(i == 0)
    def _():
        gather(tokc_ref, 0)

    @pl.when(i + 1 < nb)
    def _():
        gather(tokn_ref, 1 - slot)

    gather_wait(slot)

    @pl.when(i >= 2)
    def _():
        scatter_wait(slot)

    x = _read_tile_rows(xbuf.at[slot], MOE_TILE).astype(BF16)
    g = jnp.dot(x, wg_ref[...], preferred_element_type=F32)
    u = jnp.dot(x, wu_ref[...], preferred_element_type=F32)
    a = (g * (1.0 / (1.0 + jnp.exp(-g)))) * u
    y = jnp.dot(a.astype(BF16), wd_ref[...], preferred_element_type=F32)
    _write_tile_rows(ybuf.at[slot], y * gate_ref[...])

    whole = (base >= lo) & (base + MOE_TILE <= hi)

    @pl.when(whole)
    def _():
        for r in range(MOE_TILE):
            row_out(dst_ref[0, r], slot, r).start(priority=r % 2)

    @pl.when(jnp.logical_not(whole))
    def _():
        for r in range(MOE_TILE):
            mine = (base + r >= lo) & (base + r < hi)
            row_out(jnp.where(mine, dst_ref[0, r], junk + r), slot, r).start(priority=r % 2)

    @pl.when(i == nb - 1)
    def _():
        scatter_wait(slot)

        @pl.when(nb >= 2)
        def _():
            scatter_wait(1 - slot)


def _moe_call(h2, plan, wg, wu, wd, *, T, layer):
    vt, ve, vlo, vhi, tok, dst, gate = plan
    nb = vt.shape[0]
    smem_row = functools.partial(pl.BlockSpec, (None, 1, MOE_TILE), memory_space=pltpu.SMEM)
    buf = pltpu.VMEM((2, MOE_TILE * LANE_CHUNKS, 128), F32)
    grid_spec = pltpu.PrefetchScalarGridSpec(
        num_scalar_prefetch=4,
        grid=(nb,),
        in_specs=[
            smem_row(lambda i, vt, ve, lo, hi: (vt[i], 0, 0)),
            smem_row(lambda i, vt, ve, lo, hi: (vt[jnp.minimum(i + 1, nb - 1)], 0, 0)),
            smem_row(lambda i, vt, ve, lo, hi: (vt[i], 0, 0)),
            pl.BlockSpec((MOE_TILE, 1), lambda i, vt, ve, lo, hi: (vt[i], 0)),
            pl.BlockSpec(memory_space=pl.ANY),
            pl.BlockSpec((None, None, D_MODEL, D_EXPERT), lambda i, vt, ve, lo, hi: (layer, ve[i], 0, 0)),
            pl.BlockSpec((None, None, D_MODEL, D_EXPERT), lambda i, vt, ve, lo, hi: (layer, ve[i], 0, 0)),
            pl.BlockSpec((None, None, D_EXPERT, D_MODEL), lambda i, vt, ve, lo, hi: (layer, ve[i], 0, 0)),
        ],
        out_specs=pl.BlockSpec(memory_space=pl.ANY),
        scratch_shapes=[buf, buf, pltpu.SemaphoreType.DMA((2,)), pltpu.SemaphoreType.DMA((2,))],
    )
    return pl.pallas_call(
        functools.partial(_moe_kernel, T=T),
        grid_spec=grid_spec,
        out_shape=jax.ShapeDtypeStruct(((2 * T + 2 * MOE_TILE) * LANE_CHUNKS, 128), F32),
        compiler_params=_cparams(("arbitrary",)),
        name="moe_experts",
    )(vt, ve, vlo, vhi, tok, tok, dst, gate, h2, wg, wu, wd)


def _final_kernel(x_ref, ya_ref, yb_ref, g_ref, o_ref, *, tm):
    o_ref[...] = _rms(x_ref[...] + _read_tile_rows(ya_ref, tm) + _read_tile_rows(yb_ref, tm), g_ref[...])


def _final_call(x, y2, g, *, T, tm=512):
    nT = T // tm
    return pl.pallas_call(
        functools.partial(_final_kernel, tm=tm),
        grid=(nT,),
        in_specs=[pl.BlockSpec((tm, D_MODEL), lambda i: (i, 0)),
                  pl.BlockSpec((tm * LANE_CHUNKS, 128), lambda i: (i, 0)),
                  pl.BlockSpec((tm * LANE_CHUNKS, 128), lambda i: (nT + i, 0)),
                  pl.BlockSpec((1, D_MODEL), lambda i: (0, 0))],
        out_specs=pl.BlockSpec((tm, D_MODEL), lambda i: (i, 0)),
        out_shape=jax.ShapeDtypeStruct((T, D_MODEL), F32),
        compiler_params=_cparams(("parallel",)),
        name="final_norm",
    )(x, y2, y2, g)


def _plan(route, T):
    A = T * TOP_K
    n_tiles = A // MOE_TILE
    nb = n_tiles + N_EXPERTS - 1
    e_flat = route[:, 0:TOP_K].astype(jnp.int32).reshape(A)
    g_flat = route[:, TOP_K:2 * TOP_K].reshape(A)
    a_iota = jnp.arange(A, dtype=jnp.int32)
    _, a_s, g_s = lax.sort((e_flat, a_iota, g_flat), num_keys=1, is_stable=True)
    tok_s = a_s // TOP_K
    dst_s = (a_s % TOP_K) * T + tok_s
    eids = jnp.arange(N_EXPERTS, dtype=jnp.int32)
    counts = jnp.sum(e_flat[:, None] == eids[None, :], axis=0, dtype=jnp.int32)
    ends = jnp.cumsum(counts)
    starts = ends - counts
    first_tile = starts // MOE_TILE
    nvis = jnp.where(counts > 0, (ends - 1) // MOE_TILE - first_tile + 1, 0)
    vend = jnp.cumsum(nvis)
    vstart = vend - nvis
    v = jnp.arange(nb, dtype=jnp.int32)
    ve = jnp.minimum(jnp.sum(v[:, None] >= vend[None, :], axis=1, dtype=jnp.int32), N_EXPERTS - 1)
    onehot = (ve[:, None] == eids[None, :]).astype(jnp.int32)
    pick = lambda tab: jnp.sum(onehot * tab[None, :], axis=1, dtype=jnp.int32)
    live = v < vend[N_EXPERTS - 1]
    vt = jnp.where(live, pick(first_tile) + v - pick(vstart), n_tiles - 1)
    vlo = jnp.where(live, pick(starts), 0)
    vhi = jnp.where(live, pick(ends), 0)
    return (vt, ve, vlo, vhi, tok_s.reshape(n_tiles, 1, MOE_TILE), dst_s.reshape(n_tiles, 1, MOE_TILE),
            g_s.reshape(A, 1))


def _rope_tables(S, d, tm):
    n = tm // d
    j = jnp.arange(S, dtype=jnp.int32)
    tile, rem = j // tm, j % tm
    pos = ((tile * n + rem % n) * d + rem // n).astype(F32)
    inv_freq = 1.0 / (ROPE_THETA ** (jnp.arange(0, HEAD_DIM, 2, dtype=F32) / HEAD_DIM))
    ang = pos[:, None] * inv_freq[None, :]
    cos, sin = jnp.cos(ang), jnp.sin(ang)
    return jnp.tile(cos, (1, 4)), jnp.concatenate([-sin, sin, -sin, sin], axis=1)


def _trunk(x, p, *, B, S):
    T = B * S
    x = x.reshape(T, D_MODEL)
    tm_qkv = 512
    tables = [_rope_tables(S, d, tm_qkv) for d in DILATIONS]
    y2 = None
    for i in range(DEPTH):
        gm = p["mixer_norm"][i][None, :]
        fg = p["ffn_norm"][i][None, :]
        rw, rb = p["router_w"][i], p["router_b"][i]
        if i % 2 == 0:
            li = i // 2
            os_, ls_ = [], []
            adds = [] if y2 is None else [y2, y2]
            xs = x
            for gi_, d in enumerate(DILATIONS):
                emit = bool(adds) and gi_ == 0
                res = _qkv_call(x, adds, gm, p["w_qkv"][li][gi_], tables[gi_][0], tables[gi_][1],
                                B=B, S=S, d=d, emit_x=emit, tm=tm_qkv)
                if emit:
                    xs, qkv = res
                else:
                    (qkv,) = res
                L = S // d
                o, lse = _attn_call(qkv.reshape(B * d, L, 3 * D_MODEL), L=L)
                os_.append(o.reshape(B, d, L, D_MODEL))
                ls_.append(lse.reshape(B, d, L, 128))
            x, h2, route = _oproj_call(os_, ls_, p["w_o"][li], xs, fg, rw, rb, B=B, S=S)
        else:
            li = i // 2
            x, h2, route = _pool_call(x, y2, gm, p["w_pool"][li], p["pool_scale"][li][None, :], fg, rw, rb,
                                      B=B, S=S)
        y2 = _moe_call(h2, _plan(route, T), p["w_gate"], p["w_up"], p["w_down"], T=T, layer=i)
    out = _final_call(x, y2, p["final_norm"][None, :], T=T)
    return out.reshape(B, S, D_MODEL)


def kernel(x_prompt, x_sample, mixer_norm, w_qkv, w_o, w_pool, pool_scale, ffn_norm, router_group_w,
           router_group_b, router_expert_w, router_expert_b, w_gate, w_up, w_down, final_norm):
    n_attn = w_qkv.shape[0]
    wq = w_qkv.astype(BF16).reshape(n_attn, D_MODEL, len(DILATIONS), 3 * D_MODEL).transpose(0, 2, 1, 3)
    pad = ROUTER_LANES - N_GROUPS - N_EXPERTS
    router_w = jnp.pad(jnp.concatenate([router_group_w, router_expert_w], axis=-1),
                       ((0, 0), (0, 0), (0, pad))).astype(BF16)
    router_b = jnp.pad(jnp.concatenate([router_group_b, router_expert_b], axis=-1), ((0, 0), (0, pad)))[:, None, :]
    p = dict(mixer_norm=mixer_norm, ffn_norm=ffn_norm, final_norm=final_norm, pool_scale=pool_scale,
             w_qkv=wq, w_o=w_o.astype(BF16), w_pool=w_pool.astype(BF16),
             router_w=router_w, router_b=router_b,
             w_gate=w_gate.astype(BF16), w_up=w_up.astype(BF16), w_down=w_down.astype(BF16))
    y_prompt = _trunk(x_prompt, p, B=x_prompt.shape[0], S=x_prompt.shape[1])
    y_sample = _trunk(x_sample, p, B=x_sample.shape[0], S=x_sample.shape[1])
    return (y_prompt, y_sample)
```

```python
import functools

import jax
import jax.numpy as jnp
from jax import lax
from jax.experimental import pallas as pl
from jax.experimental.pallas import tpu as pltpu

F32 = jnp.float32
BF16 = jnp.bfloat16

D_MODEL = 1024
DEPTH = 4
DILATIONS = (1, 4, 16)
HALF = 64
N_HEADS = 16
HEAD_DIM = 64
Q_BLOCK = 128
ROPE_THETA = 10000.0
NEG_INF = -1e30
POOL_WINDOWS = (2, 4, 8, 16)
POOL_CH = 256
N_GROUPS = 4
EPG = 8
N_EXPERTS = 32
TOP_K = 2
D_EXPERT = 512
MOE_TILE = 128
RMS_EPS = 1e-6
ROUTER_LANES = 128
HALO = 8
VMEM_LIMIT = 48 * 1024 * 1024


def _rms(x, g):
    return x * lax.rsqrt(jnp.mean(x * x, axis=-1, keepdims=True) + RMS_EPS) * g


def _cparams(sem):
    return pltpu.CompilerParams(dimension_semantics=sem, vmem_limit_bytes=VMEM_LIMIT)


LANE_CHUNKS = D_MODEL // 128


def _read_tile_rows(ref, n, start=0):
    return jnp.concatenate([ref[pl.ds(start * LANE_CHUNKS + j, n, stride=LANE_CHUNKS), :]
                            for j in range(LANE_CHUNKS)], axis=1)


def _write_tile_rows(ref, val, start=0):
    n = val.shape[0]
    for j in range(LANE_CHUNKS):
        ref[pl.ds(start * LANE_CHUNKS + j, n, stride=LANE_CHUNKS), :] = val[:, 128 * j:128 * (j + 1)]


def _qkv_kernel(*refs, n_add, d, tm, emit_x):
    x_ref = refs[0]
    add_refs = refs[1:1 + n_add]
    g_ref, w_ref, cos_ref, sin_ref = refs[1 + n_add:5 + n_add]
    outs = refs[5 + n_add:]
    if emit_x:
        xs_ref, out_ref, hperm, hf = outs
    else:
        out_ref, hperm, hf = outs
    n = tm // d

    xs = x_ref[...]
    for a in add_refs:
        xs = xs + _read_tile_rows(a, tm)
    if emit_x:
        xs_ref[...] = xs
    h = _rms(xs, g_ref[...])
    if d == 1:
        hperm[...] = h.astype(BF16)
    else:
        for j in range(D_MODEL // 128):
            hf[j] = h[:, 128 * j:128 * (j + 1)]
        for r in range(d):
            for j in range(D_MODEL // 128):
                hperm[r * n:(r + 1) * n, 128 * j:128 * (j + 1)] = hf[j, pl.ds(r, n, stride=d), :].astype(BF16)

    CW = 256

    def project(col):
        return jnp.dot(hperm[...], w_ref[:, col:col + CW], preferred_element_type=F32)

    def store(col, res):
        for r in range(d):
            out_ref[r, :, col:col + res.shape[1]] = res[r * n:(r + 1) * n, :]

    lane = lax.broadcasted_iota(jnp.int32, (tm, 128), 1)
    low = (lane % HEAD_DIM) < (HEAD_DIM // 2)

    def rope(base, scale):
        for j in range(D_MODEL // CW):
            acc = project(base + CW * j)
            for k in range(CW // 128):
                xj = acc[:, 128 * k:128 * (k + 1)]
                partner = jnp.where(low, pltpu.roll(xj, 128 - HEAD_DIM // 2, 1), pltpu.roll(xj, HEAD_DIM // 2, 1))
                o = xj * cos_ref[...] + partner * sin_ref[...]
                if scale != 1.0:
                    o = o * scale
                store(base + CW * j + 128 * k, o.astype(BF16))

    rope(0, HEAD_DIM ** -0.5)
    rope(D_MODEL, 1.0)
    for j in range(D_MODEL // CW):
        store(2 * D_MODEL + CW * j, project(2 * D_MODEL + CW * j).astype(BF16))


def _qkv_call(x, adds, g, w, cos, sin, *, B, S, d, emit_x, tm=512):
    T = B * S
    nI = S // tm
    L = S // d
    n = tm // d
    n_add = len(adds)
    row = pl.BlockSpec((tm, D_MODEL), lambda i: (i, 0))
    in_specs = [row]
    T_rows = T // tm
    for k in range(n_add):
        in_specs.append(pl.BlockSpec((tm * LANE_CHUNKS, 128), lambda i, k=k: (k * T_rows + i, 0)))
    in_specs += [
        pl.BlockSpec((1, D_MODEL), lambda i: (0, 0)),
        pl.BlockSpec((D_MODEL, 3 * D_MODEL), lambda i: (0, 0)),
        pl.BlockSpec((tm, 128), lambda i: (i % nI, 0)),
        pl.BlockSpec((tm, 128), lambda i: (i % nI, 0)),
    ]
    out_spec = pl.BlockSpec((None, d, n, 3 * D_MODEL), lambda i: (i // nI, 0, i % nI, 0))
    out_shape = jax.ShapeDtypeStruct((B, d, L, 3 * D_MODEL), BF16)
    if emit_x:
        out_specs = [row, out_spec]
        out_shapes = [jax.ShapeDtypeStruct((T, D_MODEL), F32), out_shape]
    else:
        out_specs = [out_spec]
        out_shapes = [out_shape]
    res = pl.pallas_call(
        functools.partial(_qkv_kernel, n_add=n_add, d=d, tm=tm, emit_x=emit_x),
        grid=(T // tm,),
        in_specs=in_specs,
        out_specs=out_specs,
        out_shape=out_shapes,
        scratch_shapes=[pltpu.VMEM((tm, D_MODEL), BF16), pltpu.VMEM((D_MODEL // 128, tm, 128), F32)],
        compiler_params=_cparams(("parallel",)),
        name=f"qkv_d{d}",
    )(x, *adds, g, w, cos, sin)
    return res


def _attn_kernel(q_ref, kp_ref, kc_ref, kn_ref, vp_ref, vc_ref, vn_ref, o_ref, lse_ref, kbuf, vbuf, *, L):
    i = pl.program_id(1)
    W = Q_BLOCK + 2 * HALF
    kbuf[0:HALF, :] = kp_ref[...]
    kbuf[HALF:HALF + Q_BLOCK, :] = kc_ref[...]
    kbuf[HALF + Q_BLOCK:W, :] = kn_ref[...]
    vbuf[0:HALF, :] = vp_ref[...]
    vbuf[HALF:HALF + Q_BLOCK, :] = vc_ref[...]
    vbuf[HALF + Q_BLOCK:W, :] = vn_ref[...]
    QQ = 2 * Q_BLOCK
    row = lax.broadcasted_iota(jnp.int32, (QQ, W), 0) % Q_BLOCK
    col = lax.broadcasted_iota(jnp.int32, (QQ, W), 1)
    rel = col - row
    kpos = i * Q_BLOCK - HALF + col
    mask = (rel >= 0) & (rel <= 2 * HALF) & (kpos >= 0) & (kpos < L)
    lane = lax.broadcasted_iota(jnp.int32, (Q_BLOCK, 128), 1)
    first = lane < HEAD_DIM
    keep_a = first.astype(F32).astype(BF16)
    keep_b = 1 - keep_a
    lse_tile = jnp.zeros((Q_BLOCK, 128), F32)
    for j in range(N_HEADS // 2):
        sl = slice(128 * j, 128 * (j + 1))
        q2 = q_ref[:, sl]
        qq = jnp.concatenate([q2 * keep_a, q2 * keep_b], axis=0)
        s = lax.dot_general(qq, kbuf[:, sl], (((1,), (1,)), ((), ())), preferred_element_type=F32)
        s = jnp.where(mask, s, NEG_INF)
        m = jnp.max(s, axis=-1, keepdims=True)
        p = jnp.exp(s - m)
        l = jnp.sum(p, axis=-1, keepdims=True)
        o = jnp.dot(p.astype(BF16), vbuf[:, sl], preferred_element_type=F32) / l
        o_ref[:, sl] = jnp.where(first, o[:Q_BLOCK], o[Q_BLOCK:]).astype(BF16)
        lse = m + jnp.log(l)
        lse_tile = jnp.where(lane == 2 * j, lse[:Q_BLOCK],
                             jnp.where(lane == 2 * j + 1, lse[Q_BLOCK:], lse_tile))
    lse_ref[...] = lse_tile


def _attn_call(qkv, *, L):
    NS = qkv.shape[0]
    nq = L // Q_BLOCK
    nh = L // HALF
    q_spec = pl.BlockSpec((None, Q_BLOCK, D_MODEL), lambda s, i: (s, i, 0))

    def cur(cb):
        return pl.BlockSpec((None, Q_BLOCK, D_MODEL), lambda s, i: (s, i, cb))

    def prev(cb):
        return pl.BlockSpec((None, HALF, D_MODEL), lambda s, i: (s, jnp.maximum(2 * i - 1, 0), cb))

    def nxt(cb):
        return pl.BlockSpec((None, HALF, D_MODEL), lambda s, i: (s, jnp.minimum(2 * i + 2, nh - 1), cb))

    return pl.pallas_call(
        functools.partial(_attn_kernel, L=L),
        grid=(NS, nq),
        in_specs=[q_spec, prev(1), cur(1), nxt(1), prev(2), cur(2), nxt(2)],
        out_specs=[pl.BlockSpec((None, Q_BLOCK, D_MODEL), lambda s, i: (s, i, 0)),
                   pl.BlockSpec((None, Q_BLOCK, 128), lambda s, i: (s, i, 0))],
        out_shape=[jax.ShapeDtypeStruct((NS, L, D_MODEL), BF16),
                   jax.ShapeDtypeStruct((NS, L, 128), F32)],
        scratch_shapes=[pltpu.VMEM((Q_BLOCK + 2 * HALF, D_MODEL), BF16),
                        pltpu.VMEM((Q_BLOCK + 2 * HALF, D_MODEL), BF16)],
        compiler_params=_cparams(("parallel", "parallel")),
        name="band_attn",
    )(qkv, qkv, qkv, qkv, qkv, qkv, qkv)


def _post(x_new, fg_ref, rw_ref, rb_ref, xn_ref, h2_ref, rt_ref):
    xn_ref[...] = x_new
    h2 = _rms(x_new, fg_ref[...])
    _write_tile_rows(h2_ref, h2)
    lg = jnp.dot(h2.astype(BF16), rw_ref[...], preferred_element_type=F32) + rb_ref[...]
    lane = lax.broadcasted_iota(jnp.int32, lg.shape, 1)

    def first_argmax(v, vmax):
        return jnp.min(jnp.where(v == vmax, lane, ROUTER_LANES), axis=-1, keepdims=True)

    is_g = lane < N_GROUPS
    vg = jnp.where(is_g, lg, -jnp.inf)
    gmax = jnp.max(vg, axis=-1, keepdims=True)
    gi = first_argmax(vg, gmax)
    gp = 1.0 / jnp.sum(jnp.where(is_g, jnp.exp(lg - gmax), 0.0), axis=-1, keepdims=True)
    lo = N_GROUPS + EPG * gi
    in_grp = (lane >= lo) & (lane < lo + EPG)
    ve = jnp.where(in_grp, lg, -jnp.inf)
    m1 = jnp.max(ve, axis=-1, keepdims=True)
    i1 = first_argmax(ve, m1)
    ve2 = jnp.where(lane == i1, -jnp.inf, ve)
    m2 = jnp.max(ve2, axis=-1, keepdims=True)
    i2 = first_argmax(ve2, m2)
    esum = jnp.sum(jnp.where(in_grp, jnp.exp(lg - m1), 0.0), axis=-1, keepdims=True)
    ep1 = 1.0 / esum
    ep2 = jnp.exp(m2 - m1) / esum
    den = ep1 + ep2
    g1 = gp * ep1 / den
    g2 = gp * ep2 / den
    e1 = (i1 - N_GROUPS).astype(F32)
    e2 = (i2 - N_GROUPS).astype(F32)
    rt_ref[...] = jnp.where(lane == 0, e1, jnp.where(lane == 1, e2, jnp.where(lane == 2, g1,
                            jnp.where(lane == 3, g2, 0.0))))


def _oproj_kernel(o0, o1, o2, l0, l1, l2, wo_ref, x_ref, fg_ref, rw_ref, rb_ref,
                  xn_ref, h2_ref, lg_ref, onat, lnat, mbuf, *, tm):
    o_refs = (o0, o1, o2)
    l_refs = (l0, l1, l2)
    for g, d in enumerate(DILATIONS):
        n = tm // d
        for r in range(d):
            if d == 1:
                lnat[g] = l_refs[g][r]
                for j in range(D_MODEL // 128):
                    onat[g, j] = o_refs[g][r, :, 128 * j:128 * (j + 1)].astype(F32)
            else:
                lnat[g, pl.ds(r, n, stride=d), :] = l_refs[g][r]
                for j in range(D_MODEL // 128):
                    onat[g, j, pl.ds(r, n, stride=d), :] = o_refs[g][r, :, 128 * j:128 * (j + 1)].astype(F32)
    ls = [lnat[g] for g in range(3)]
    m = jnp.maximum(jnp.maximum(ls[0], ls[1]), ls[2])
    es = [jnp.exp(l - m) for l in ls]
    tot = es[0] + es[1] + es[2]
    al = [e / tot for e in es]
    lane = lax.broadcasted_iota(jnp.int32, (tm, 128), 1)
    first = lane < HEAD_DIM
    for j in range(D_MODEL // 128):
        acc = None
        for g in range(3):
            a = jnp.where(first, al[g][:, 2 * j:2 * j + 1], al[g][:, 2 * j + 1:2 * j + 2])
            t = a * onat[g, j]
            acc = t if acc is None else acc + t
        mbuf[:, 128 * j:128 * (j + 1)] = acc.astype(BF16)
    y = jnp.dot(mbuf[...], wo_ref[...], preferred_element_type=F32)
    _post(x_ref[...] + y, fg_ref, rw_ref, rb_ref, xn_ref, h2_ref, lg_ref)


def _oproj_call(os_, ls_, wo, x, fg, rw, rb, *, B, S, tm=256):
    T = B * S
    nI = S // tm
    in_specs = []
    for d in DILATIONS:
        in_specs.append(pl.BlockSpec((None, d, tm // d, D_MODEL), lambda i: (i // nI, 0, i % nI, 0)))
    for d in DILATIONS:
        in_specs.append(pl.BlockSpec((None, d, tm // d, 128), lambda i: (i // nI, 0, i % nI, 0)))
    row = pl.BlockSpec((tm, D_MODEL), lambda i: (i, 0))
    in_specs += [
        pl.BlockSpec((D_MODEL, D_MODEL), lambda i: (0, 0)),
        row,
        pl.BlockSpec((1, D_MODEL), lambda i: (0, 0)),
        pl.BlockSpec((D_MODEL, ROUTER_LANES), lambda i: (0, 0)),
        pl.BlockSpec((1, ROUTER_LANES), lambda i: (0, 0)),
    ]
    return pl.pallas_call(
        functools.partial(_oproj_kernel, tm=tm),
        grid=(T // tm,),
        in_specs=in_specs,
        out_specs=[row, pl.BlockSpec((tm * LANE_CHUNKS, 128), lambda i: (i, 0)),
                   pl.BlockSpec((tm, ROUTER_LANES), lambda i: (i, 0))],
        out_shape=[jax.ShapeDtypeStruct((T, D_MODEL), F32), jax.ShapeDtypeStruct((T * LANE_CHUNKS, 128), F32),
                   jax.ShapeDtypeStruct((T, ROUTER_LANES), F32)],
        scratch_shapes=[pltpu.VMEM((3, D_MODEL // 128, tm, 128), F32), pltpu.VMEM((3, tm, 128), F32),
                        pltpu.VMEM((tm, D_MODEL), BF16)],
        compiler_params=_cparams(("parallel",)),
        name="merge_oproj",
    )(*os_, *ls_, wo, x, fg, rw, rb)


def _pool_kernel(x_ref, ya_ref, yb_ref, xp_ref, yap_ref, ybp_ref, xq_ref, yaq_ref, ybq_ref,
                 g_ref, wp_ref, ps_ref, fg_ref, rw_ref, rb_ref, xn_ref, h2_ref, lg_ref, hp, xnew, *, tm, S):
    i = pl.program_id(0)
    nI = S // tm
    ii = i % nI
    g = g_ref[...]
    xs = x_ref[...] + _read_tile_rows(ya_ref, tm) + _read_tile_rows(yb_ref, tm)
    h = _rms(xs, g)
    hprev = _rms(xp_ref[...] + _read_tile_rows(yap_ref, HALO) + _read_tile_rows(ybp_ref, HALO), g)
    hnext = _rms(xq_ref[...] + _read_tile_rows(yaq_ref, HALO) + _read_tile_rows(ybq_ref, HALO), g)
    hp[0:HALO, :] = jnp.where(ii > 0, hprev, 0.0)
    hp[HALO:HALO + tm, :] = h
    hp[HALO + tm:2 * HALO + tm, :] = jnp.where(ii < nI - 1, hnext, 0.0)
    t = ii * tm + lax.broadcasted_iota(jnp.int32, (tm, 1), 0)
    for k, w in enumerate(POOL_WINDOWS):
        cols = slice(k * POOL_CH, (k + 1) * POOL_CH)
        acc = None
        for j in range(-(w // 2), w // 2):
            v = hp[HALO + j:HALO + j + tm, cols]
            acc = v if acc is None else acc + v
        lo = jnp.clip(t - w // 2, 0, S)
        hi = jnp.clip(t - w // 2 + w, 0, S)
        mean = acc / (hi - lo).astype(F32)
        diff = (mean - h[:, cols]).astype(BF16)
        out = jnp.dot(diff, wp_ref[k], preferred_element_type=F32)
        xnew[:, cols] = xs[:, cols] + out * ps_ref[:, cols]
    _post(xnew[...], fg_ref, rw_ref, rb_ref, xn_ref, h2_ref, lg_ref)


def _pool_call(x, y2, g, wp, ps, fg, rw, rb, *, B, S, tm=512):
    T = B * S
    nb8 = T // HALO
    r8 = tm // HALO
    nT = T // tm

    def main(k, shape=(tm, D_MODEL)):
        return pl.BlockSpec(shape, lambda i: (k * nT + i, 0))

    def prev(k, shape=(HALO, D_MODEL)):
        return pl.BlockSpec(shape, lambda i: (k * nb8 + jnp.maximum(i * r8 - 1, 0), 0))

    def nxt(k, shape=(HALO, D_MODEL)):
        return pl.BlockSpec(shape, lambda i: (k * nb8 + jnp.minimum((i + 1) * r8, nb8 - 1), 0))

    row = pl.BlockSpec((tm, D_MODEL), lambda i: (i, 0))
    vec = pl.BlockSpec((1, D_MODEL), lambda i: (0, 0))
    tr_main = (tm * LANE_CHUNKS, 128)
    tr_halo = (HALO * LANE_CHUNKS, 128)
    in_specs = [main(0), main(0, tr_main), main(1, tr_main), prev(0), prev(0, tr_halo), prev(1, tr_halo),
                nxt(0), nxt(0, tr_halo), nxt(1, tr_halo),
                vec, pl.BlockSpec((len(POOL_WINDOWS), POOL_CH, POOL_CH), lambda i: (0, 0, 0)), vec, vec,
                pl.BlockSpec((D_MODEL, ROUTER_LANES), lambda i: (0, 0)),
                pl.BlockSpec((1, ROUTER_LANES), lambda i: (0, 0))]
    return pl.pallas_call(
        functools.partial(_pool_kernel, tm=tm, S=S),
        grid=(nT,),
        in_specs=in_specs,
        out_specs=[row, pl.BlockSpec(tr_main, lambda i: (i, 0)),
                   pl.BlockSpec((tm, ROUTER_LANES), lambda i: (i, 0))],
        out_shape=[jax.ShapeDtypeStruct((T, D_MODEL), F32), jax.ShapeDtypeStruct((T * LANE_CHUNKS, 128), F32),
                   jax.ShapeDtypeStruct((T, ROUTER_LANES), F32)],
        scratch_shapes=[pltpu.VMEM((tm + 2 * HALO, D_MODEL), F32), pltpu.VMEM((tm, D_MODEL), F32)],
        compiler_params=_cparams(("parallel",)),
        name="pool_mixer",
    )(x, y2, y2, x, y2, y2, x, y2, y2, g, wp, ps, fg, rw, rb)


def _moe_kernel(vt_ref, ve_ref, vlo_ref, vhi_ref, tokc_ref, tokn_ref, dst_ref, gate_ref, h_hbm,
                wg_ref, wu_ref, wd_ref, y_hbm, xbuf, ybuf, gsem, ssem, *, T):
    del ve_ref
    i = pl.program_id(0)
    nb = pl.num_programs(0)
    slot = i % 2
    base = vt_ref[i] * MOE_TILE
    lo = vlo_ref[i]
    hi = vhi_ref[i]
    junk = 2 * T + slot * MOE_TILE
    RC = LANE_CHUNKS

    def row_in(tok, s, r):
        return pltpu.make_async_copy(h_hbm.at[pl.ds(pl.multiple_of(tok * RC, RC), RC), :],
                                     xbuf.at[s, pl.ds(r * RC, RC), :], gsem.at[s])

    def row_out(row, s, r):
        return pltpu.make_async_copy(ybuf.at[s, pl.ds(r * RC, RC), :],
                                     y_hbm.at[pl.ds(pl.multiple_of(row * RC, RC), RC), :], ssem.at[s])

    def gather(tok_ref, s):
        for r in range(MOE_TILE):
            row_in(tok_ref[0, r], s, r).start(priority=r % 2)

    def gather_wait(s):
        for r in range(MOE_TILE):
            row_in(0, s, r).wait()

    def scatter_wait(s):
        for r in range(MOE_TILE):
            row_out(0, s, r).wait()

    @pl.when(i == 0)
    def _():
        gather(tokc_ref, 0)

    @pl.when(i + 1 < nb)
    def _():
        gather(tokn_ref, 1 - slot)

    gather_wait(slot)

    @pl.when(i >= 2)
    def _():
        scatter_wait(slot)

    x = _read_tile_rows(xbuf.at[slot], MOE_TILE).astype(BF16)
    g = jnp.dot(x, wg_ref[...], preferred_element_type=F32)
    u = jnp.dot(x, wu_ref[...], preferred_element_type=F32)
    a = (g * (1.0 / (1.0 + jnp.exp(-g)))) * u
    y = jnp.dot(a.astype(BF16), wd_ref[...], preferred_element_type=F32)
    _write_tile_rows(ybuf.at[slot], y * gate_ref[...])

    whole = (base >= lo) & (base + MOE_TILE <= hi)

    @pl.when(whole)
    def _():
        for r in range(MOE_TILE):
            row_out(dst_ref[0, r], slot, r).start(priority=r % 2)

    @pl.when(jnp.logical_not(whole))
    def _():
        for r in range(MOE_TILE):
            mine = (base + r >= lo) & (base + r < hi)
            row_out(jnp.where(mine, dst_ref[0, r], junk + r), slot, r).start(priority=r % 2)

    @pl.when(i == nb - 1)
    def _():
        scatter_wait(slot)

        @pl.when(nb >= 2)
        def _():
            scatter_wait(1 - slot)


def _moe_call(h2, plan, wg, wu, wd, *, T, layer):
    vt, ve, vlo, vhi, tok, dst, gate = plan
    nb = vt.shape[0]
    smem_row = functools.partial(pl.BlockSpec, (None, 1, MOE_TILE), memory_space=pltpu.SMEM)
    buf = pltpu.VMEM((2, MOE_TILE * LANE_CHUNKS, 128), F32)
    grid_spec = pltpu.PrefetchScalarGridSpec(
        num_scalar_prefetch=4,
        grid=(nb,),
        in_specs=[
            smem_row(lambda i, vt, ve, lo, hi: (vt[i], 0, 0)),
            smem_row(lambda i, vt, ve, lo, hi: (vt[jnp.minimum(i + 1, nb - 1)], 0, 0)),
            smem_row(lambda i, vt, ve, lo, hi: (vt[i], 0, 0)),
            pl.BlockSpec((MOE_TILE, 1), lambda i, vt, ve, lo, hi: (vt[i], 0)),
            pl.BlockSpec(memory_space=pl.ANY),
            pl.BlockSpec((None, None, D_MODEL, D_EXPERT), lambda i, vt, ve, lo, hi: (layer, ve[i], 0, 0)),
            pl.BlockSpec((None, None, D_MODEL, D_EXPERT), lambda i, vt, ve, lo, hi: (layer, ve[i], 0, 0)),
            pl.BlockSpec((None, None, D_EXPERT, D_MODEL), lambda i, vt, ve, lo, hi: (layer, ve[i], 0, 0)),
        ],
        out_specs=pl.BlockSpec(memory_space=pl.ANY),
        scratch_shapes=[buf, buf, pltpu.SemaphoreType.DMA((2,)), pltpu.SemaphoreType.DMA((2,))],
    )
    return pl.pallas_call(
        functools.partial(_moe_kernel, T=T),
        grid_spec=grid_spec,
        out_shape=jax.ShapeDtypeStruct(((2 * T + 2 * MOE_TILE) * LANE_CHUNKS, 128), F32),
        compiler_params=_cparams(("arbitrary",)),
        name="moe_experts",
    )(vt, ve, vlo, vhi, tok, tok, dst, gate, h2, wg, wu, wd)


def _final_kernel(x_ref, ya_ref, yb_ref, g_ref, o_ref, *, tm):
    o_ref[...] = _rms(x_ref[...] + _read_tile_rows(ya_ref, tm) + _read_tile_rows(yb_ref, tm), g_ref[...])


def _final_call(x, y2, g, *, T, tm=512):
    nT = T // tm
    return pl.pallas_call(
        functools.partial(_final_kernel, tm=tm),
        grid=(nT,),
        in_specs=[pl.BlockSpec((tm, D_MODEL), lambda i: (i, 0)),
                  pl.BlockSpec((tm * LANE_CHUNKS, 128), lambda i: (i, 0)),
                  pl.BlockSpec((tm * LANE_CHUNKS, 128), lambda i: (nT + i, 0)),
                  pl.BlockSpec((1, D_MODEL), lambda i: (0, 0))],
        out_specs=pl.BlockSpec((tm, D_MODEL), lambda i: (i, 0)),
        out_shape=jax.ShapeDtypeStruct((T, D_MODEL), F32),
        compiler_params=_cparams(("parallel",)),
        name="final_norm",
    )(x, y2, y2, g)


def _plan(route, T):
    A = T * TOP_K
    n_tiles = A // MOE_TILE
    nb = n_tiles + N_EXPERTS - 1
    e_flat = route[:, 0:TOP_K].astype(jnp.int32).reshape(A)
    g_flat = route[:, TOP_K:2 * TOP_K].reshape(A)
    a_iota = jnp.arange(A, dtype=jnp.int32)
    _, a_s, g_s = lax.sort((e_flat, a_iota, g_flat), num_keys=1, is_stable=True)
    tok_s = a_s // TOP_K
    dst_s = (a_s % TOP_K) * T + tok_s
    eids = jnp.arange(N_EXPERTS, dtype=jnp.int32)
    counts = jnp.sum(e_flat[:, None] == eids[None, :], axis=0, dtype=jnp.int32)
    ends = jnp.cumsum(counts)
    starts = ends - counts
    first_tile = starts // MOE_TILE
    nvis = jnp.where(counts > 0, (ends - 1) // MOE_TILE - first_tile + 1, 0)
    vend = jnp.cumsum(nvis)
    vstart = vend - nvis
    v = jnp.arange(nb, dtype=jnp.int32)
    ve = jnp.minimum(jnp.sum(v[:, None] >= vend[None, :], axis=1, dtype=jnp.int32), N_EXPERTS - 1)
    onehot = (ve[:, None] == eids[None, :]).astype(jnp.int32)
    pick = lambda tab: jnp.sum(onehot * tab[None, :], axis=1, dtype=jnp.int32)
    live = v < vend[N_EXPERTS - 1]
    vt = jnp.where(live, pick(first_tile) + v - pick(vstart), n_tiles - 1)
    vlo = jnp.where(live, pick(starts), 0)
    vhi = jnp.where(live, pick(ends), 0)
    return (vt, ve, vlo, vhi, tok_s.reshape(n_tiles, 1, MOE_TILE), dst_s.reshape(n_tiles, 1, MOE_TILE),
            g_s.reshape(A, 1))


def _rope_tables(S, d, tm):
    n = tm // d
    j = jnp.arange(S, dtype=jnp.int32)
    tile, rem = j // tm, j % tm
    pos = ((tile * n + rem % n) * d + rem // n).astype(F32)
    inv_freq = 1.0 / (ROPE_THETA ** (jnp.arange(0, HEAD_DIM, 2, dtype=F32) / HEAD_DIM))
    ang = pos[:, None] * inv_freq[None, :]
    cos, sin = jnp.cos(ang), jnp.sin(ang)
    return jnp.tile(cos, (1, 4)), jnp.concatenate([-sin, sin, -sin, sin], axis=1)


def _trunk(x, p, *, B, S):
    T = B * S
    x = x.reshape(T, D_MODEL)
    tm_qkv = 512
    tables = [_rope_tables(S, d, tm_qkv) for d in DILATIONS]
    y2 = None
    for i in range(DEPTH):
        gm = p["mixer_norm"][i][None, :]
        fg = p["ffn_norm"][i][None, :]
        rw, rb = p["router_w"][i], p["router_b"][i]
        if i % 2 == 0:
            li = i // 2
            os_, ls_ = [], []
            adds = [] if y2 is None else [y2, y2]
            xs = x
            for gi_, d in enumerate(DILATIONS):
                emit = bool(adds) and gi_ == 0
                res = _qkv_call(x, adds, gm, p["w_qkv"][li][gi_], tables[gi_][0], tables[gi_][1],
                                B=B, S=S, d=d, emit_x=emit, tm=tm_qkv)
                if emit:
                    xs, qkv = res
                else:
                    (qkv,) = res
                L = S // d
                o, lse = _attn_call(qkv.reshape(B * d, L, 3 * D_MODEL), L=L)
                os_.append(o.reshape(B, d, L, D_MODEL))
                ls_.append(lse.reshape(B, d, L, 128))
            x, h2, route = _oproj_call(os_, ls_, p["w_o"][li], xs, fg, rw, rb, B=B, S=S)
        else:
            li = i // 2
            x, h2, route = _pool_call(x, y2, gm, p["w_pool"][li], p["pool_scale"][li][None, :], fg, rw, rb,
                                      B=B, S=S)
        y2 = _moe_call(h2, _plan(route, T), p["w_gate"], p["w_up"], p["w_down"], T=T, layer=i)
    out = _final_call(x, y2, p["final_norm"][None, :], T=T)
    return out.reshape(B, S, D_MODEL)


def kernel(x_prompt, x_sample, mixer_norm, w_qkv, w_o, w_pool, pool_scale, ffn_norm, router_group_w,
           router_group_b, router_expert_w, router_expert_b, w_gate, w_up, w_down, final_norm):
    n_attn = w_qkv.shape[0]
    wq = w_qkv.astype(BF16).reshape(n_attn, D_MODEL, len(DILATIONS), 3 * D_MODEL).transpose(0, 2, 1, 3)
    pad = ROUTER_LANES - N_GROUPS - N_EXPERTS
    router_w = jnp.pad(jnp.concatenate([router_group_w, router_expert_w], axis=-1),
                       ((0, 0), (0, 0), (0, pad))).astype(BF16)
    router_b = jnp.pad(jnp.concatenate([router_group_b, router_expert_b], axis=-1), ((0, 0), (0, pad)))[:, None, :]
    p = dict(mixer_norm=mixer_norm, ffn_norm=ffn_norm, final_norm=final_norm, pool_scale=pool_scale,
             w_qkv=wq, w_o=w_o.astype(BF16), w_pool=w_pool.astype(BF16),
             router_w=router_w, router_b=router_b,
             w_gate=w_gate.astype(BF16), w_up=w_up.astype(BF16), w_down=w_down.astype(BF16))
    y_prompt = _trunk(x_prompt, p, B=x_prompt.shape[0], S=x_prompt.shape[1])
    y_sample = _trunk(x_sample, p, B=x_sample.shape[0], S=x_sample.shape[1])
    return (y_prompt, y_sample)
```
